```python
import jax
import jax.numpy as jnp
from jax import lax
import numpy as np

D_MODEL = 4096
BATCH = 2
SEQ = 4096
DEPTH = 1

MEM_LEN = 256
HEAD_DIM = 128
A_WIDTH = 3 * D_MODEL // 8
A_HEADS = A_WIDTH // HEAD_DIM
DILATED_PATTERNS = ((128, 1), (512, 4), (2048, 16))
B_WIDTH = 3 * D_MODEL // 8
B_KEY_DIM = 128
B_VAL_DIM = 128
B_HEADS = B_WIDTH // B_KEY_DIM
HGRN_CHUNK = 64
M_WIDTH = D_MODEL // 4
M_HEADS = 4
M_HEAD_DIM = M_WIDTH // M_HEADS
MIX_WIDTH = A_WIDTH + B_WIDTH + M_WIDTH
D_FF = ((8 * D_MODEL // 3 + 255) // 256) * 256
CONV_WIDTH = 3
ROPE_THETA = 10000.0
LN_EPS = 1e-5
RMS_EPS = 1e-6
NEG_INF = -1e30
DEEPNORM_ALPHA = (2 * DEPTH) ** 0.25
DEEPNORM_BETA = (8 * DEPTH) ** -0.25
IN_BLOCKS = (A_WIDTH, A_WIDTH, A_WIDTH, B_WIDTH, B_WIDTH, B_WIDTH, B_WIDTH, B_WIDTH, M_WIDTH)
IN_COLS = sum(IN_BLOCKS)
IN_SPLITS = tuple(int(v) for v in np.cumsum(IN_BLOCKS)[:-1])

kernel_name = "hymba_dilated_hgrn2_memory_deepnorm_encoder"


def _layernorm(x, g, b):
    xf = x.astype(jnp.float32)
    mu = jnp.mean(xf, axis=-1, keepdims=True)
    var = jnp.mean(jnp.square(xf - mu), axis=-1, keepdims=True)
    return ((xf - mu) * lax.rsqrt(var + LN_EPS) * g.astype(jnp.float32) + b.astype(jnp.float32)).astype(x.dtype)


def _rope_tables(positions):
    inv = 1.0 / (ROPE_THETA ** (jnp.arange(0, HEAD_DIM, 2, dtype=jnp.float32) / HEAD_DIM))
    ang = positions.astype(jnp.float32)[..., None] * inv
    return jnp.cos(ang)[:, :, None, :], jnp.sin(ang)[:, :, None, :]


def _rope(t, cos, sin):
    half = t.shape[-1] // 2
    t1 = t[..., :half].astype(jnp.float32)
    t2 = t[..., half:].astype(jnp.float32)
    return jnp.concatenate([t1 * cos - t2 * sin, t2 * cos + t1 * sin], axis=-1).astype(t.dtype)


def _banded_attention(q, k, v, half):
    L, hd = q.shape[-2], q.shape[-1]
    lead = q.shape[:-2]
    blk = half
    nb = -(-L // blk)
    lp = nb * blk
    pad_end = [(0, 0)] * len(lead) + [(0, lp - L), (0, 0)]
    pad_both = [(0, 0)] * len(lead) + [(blk, lp - L + blk), (0, 0)]
    qb = jnp.pad(q, pad_end).reshape(*lead, nb, blk, hd)

    def windows(t):
        tb = jnp.pad(t, pad_both).reshape(*lead, nb + 2, blk, hd)
        return jnp.concatenate([tb[..., :-2, :, :], tb[..., 1:-1, :, :], tb[..., 2:, :, :]], axis=-2)

    kw, vw = windows(k), windows(v)
    s = jnp.einsum('...nqd,...nkd->...nqk', qb, kw).astype(jnp.float32)
    qpos = jnp.arange(nb)[:, None, None] * blk + jnp.arange(blk)[None, :, None]
    kpos = (jnp.arange(nb)[:, None, None] - 1) * blk + jnp.arange(3 * blk)[None, None, :]
    valid = (jnp.abs(qpos - kpos) <= half) & (kpos >= 0) & (kpos < L)
    s = jnp.where(valid, s, NEG_INF)
    lse = jax.nn.logsumexp(s, axis=-1)
    p = jnp.exp(s - lse[..., None])
    o = jnp.einsum('...nqk,...nkd->...nqd', p.astype(v.dtype), vw)
    o = o.reshape(*lead, lp, hd)[..., :L, :]
    lse = lse.reshape(*lead, lp)[..., :L]
    return o, lse


def _dilated_branch(q, k, v, window, dilation):
    b, h, s, hd = q.shape
    L = s // dilation

    def split(t):
        return t.reshape(b, h, L, dilation, hd).transpose(0, 1, 3, 2, 4)

    o, lse = _banded_attention(split(q), split(k), split(v), window // (2 * dilation))
    o = o.transpose(0, 1, 3, 2, 4).reshape(b, h, s, hd)
    lse = lse.transpose(0, 1, 3, 2).reshape(b, h, s)
    return o, lse


def _dilated_attention(q, k, v):
    outs, lses = [], []
    for window, dilation in DILATED_PATTERNS:
        o, lse = _dilated_branch(q, k, v, window, dilation)
        outs.append(o.astype(jnp.float32))
        lses.append(lse)
    w = jax.nn.softmax(jnp.stack(lses), axis=0)
    return jnp.sum(w[..., None] * jnp.stack(outs), axis=0)


def _hgrn2_bidirectional(q, v, f_fwd, f_bwd):
    qs = jnp.stack([q, q[:, ::-1]])
    ks = jnp.stack([1.0 - f_fwd, (1.0 - f_bwd)[:, ::-1]])
    vs = jnp.stack([v, v[:, ::-1]])
    gs = jnp.stack([jnp.log(f_fwd), jnp.log(f_bwd)[:, ::-1]])
    _, b, s, h, dk = qs.shape
    dv = vs.shape[-1]
    n = s // HGRN_CHUNK

    def chunks(t):
        return t.reshape(2, b, n, HGRN_CHUNK, h, t.shape[-1]).transpose(2, 0, 1, 4, 3, 5)

    tri = jnp.tril(jnp.ones((HGRN_CHUNK, HGRN_CHUNK), dtype=bool))

    def step(state, inp):
        qc, kc, vc, gc = inp
        bc = jnp.cumsum(gc, axis=-2)
        inter = jnp.einsum('...tk,...kv->...tv', qc * jnp.exp(bc), state)
        diff = bc[..., :, None, :] - bc[..., None, :, :]
        decay = jnp.exp(jnp.where(tri[..., None], diff, -jnp.inf))
        scores = jnp.einsum('...tk,...sk,...tsk->...ts', qc, kc, decay)
        intra = jnp.einsum('...ts,...sv->...tv', scores, vc)
        last = bc[..., -1:, :]
        new_state = (jnp.exp(last[..., 0, :])[..., None] * state
                     + jnp.einsum('...sk,...sv->...kv', kc * jnp.exp(last - bc), vc))
        return new_state, inter + intra

    init = jnp.zeros((2, b, h, dk, dv), jnp.float32)
    _, ys = lax.scan(step, init, (chunks(qs), chunks(ks), chunks(vs), chunks(gs)))
    ys = ys.transpose(1, 2, 0, 4, 3, 5).reshape(2, b, s, h, dv)
    return ys[0] + ys[1][:, ::-1]


def _lower_bound(gamma, layer):
    p = jax.nn.softmax(gamma.astype(jnp.float32), axis=0)
    return jnp.cumsum(p, axis=0)[layer]


def _token_mixers(h, mem, cos, sin, w_in, w_mem_kv, w_o, lb_fwd, lb_bwd, norm_g):
    b, s, _ = h.shape
    f32 = jnp.float32
    proj = h @ w_in
    qa, ka, va, qb, ib, gb, fpf, fpb, qm = jnp.split(proj, IN_SPLITS, axis=-1)

    qa = _rope(qa.reshape(b, s, A_HEADS, HEAD_DIM), cos, sin) * (HEAD_DIM ** -0.5)
    ka = _rope(ka.reshape(b, s, A_HEADS, HEAD_DIM), cos, sin)
    va = va.reshape(b, s, A_HEADS, HEAD_DIM)
    o_a = _dilated_attention(qa.transpose(0, 2, 1, 3), ka.transpose(0, 2, 1, 3), va.transpose(0, 2, 1, 3))
    o_a = o_a.transpose(0, 2, 1, 3).reshape(b, s, A_WIDTH).astype(h.dtype)

    def bh(t, d):
        return t.reshape(b, s, B_HEADS, d).astype(f32)
    lbf = lb_fwd.reshape(B_HEADS, B_KEY_DIM)
    lbb = lb_bwd.reshape(B_HEADS, B_KEY_DIM)
    f_fwd = lbf + (1.0 - lbf) * jax.nn.sigmoid(bh(fpf, B_KEY_DIM))
    f_bwd = lbb + (1.0 - lbb) * jax.nn.sigmoid(bh(fpb, B_KEY_DIM))
    o_b = _hgrn2_bidirectional(jax.nn.silu(bh(qb, B_KEY_DIM)), bh(ib, B_VAL_DIM), f_fwd, f_bwd)
    o_b = (o_b * lax.rsqrt(jnp.mean(jnp.square(o_b), axis=-1, keepdims=True) + RMS_EPS)
           * norm_g.astype(f32) * jax.nn.silu(bh(gb, B_VAL_DIM)))
    o_b = o_b.reshape(b, s, B_WIDTH).astype(h.dtype)

    m = mem.shape[1]
    kvm = mem @ w_mem_kv
    km = kvm[..., :M_WIDTH].reshape(b, m, M_HEADS, M_HEAD_DIM)
    vm = kvm[..., M_WIDTH:].reshape(b, m, M_HEADS, M_HEAD_DIM)
    qm = qm.reshape(b, s, M_HEADS, M_HEAD_DIM) * (M_HEAD_DIM ** -0.5)
    pm = jax.nn.softmax(jnp.einsum('bshd,bmhd->bhsm', qm, km).astype(f32), axis=-1)
    o_m = jnp.einsum('bhsm,bmhd->bshd', pm.astype(vm.dtype), vm).reshape(b, s, M_WIDTH)

    return jnp.concatenate([o_a, o_b, o_m.astype(h.dtype)], axis=-1) @ w_o


def _conv_ffn(h, w_up, conv_w, conv_b, w_down):
    s = h.shape[1]
    up = h @ w_up
    g, u = up[..., :D_FF], up[..., D_FF:]
    pad = CONV_WIDTH // 2
    gp = jnp.pad(g, ((0, 0), (pad, pad), (0, 0)))
    gc = conv_b
    for j in range(CONV_WIDTH):
        gc = gc + gp[:, j:j + s, :] * conv_w[j]
    return (jax.nn.silu(gc) * u) @ w_down


def setup_inputs(seed: int = 0) -> dict:
    key = jax.random.key(seed)
    ks = jax.random.split(key, 20)
    f32 = jnp.float32

    def nrm(k, shape, scale):
        return jax.random.normal(k, shape, f32) * scale

    beta = DEEPNORM_BETA
    col_scale = jnp.concatenate([
        jnp.ones((2 * A_WIDTH,), f32), jnp.full((A_WIDTH,), beta, f32),
        jnp.ones((B_WIDTH,), f32), jnp.full((B_WIDTH,), beta, f32),
        jnp.ones((3 * B_WIDTH + M_WIDTH,), f32)])
    kv_scale = jnp.concatenate([jnp.ones((M_WIDTH,), f32), jnp.full((M_WIDTH,), beta, f32)])
    return {
        "x": nrm(ks[0], (BATCH, SEQ, D_MODEL), 1.0),
        "mem": nrm(ks[1], (BATCH, MEM_LEN, D_MODEL), 1.0),
        "positions": jnp.broadcast_to(jnp.arange(SEQ, dtype=jnp.int32)[None, :], (BATCH, SEQ)),
        "w_in": nrm(ks[2], (DEPTH, D_MODEL, IN_COLS), D_MODEL ** -0.5) * col_scale,
        "w_mem_kv": nrm(ks[3], (DEPTH, D_MODEL, 2 * M_WIDTH), D_MODEL ** -0.5) * kv_scale,
        "w_o": nrm(ks[4], (DEPTH, MIX_WIDTH, D_MODEL), beta * MIX_WIDTH ** -0.5),
        "hgrn_gamma_fwd": nrm(ks[5], (DEPTH + 1, B_WIDTH), 0.1),
        "hgrn_gamma_bwd": nrm(ks[6], (DEPTH + 1, B_WIDTH), 0.1),
        "hgrn_norm_g": 1.0 + nrm(ks[7], (DEPTH, B_HEADS, B_VAL_DIM), 0.02),
        "ln1_g": 1.0 + nrm(ks[8], (DEPTH, D_MODEL), 0.02),
        "ln1_b": nrm(ks[9], (DEPTH, D_MODEL), 0.02),
        "w_up": nrm(ks[10], (DEPTH, D_MODEL, 2 * D_FF), D_MODEL ** -0.5),
        "conv_w": nrm(ks[11], (DEPTH, CONV_WIDTH, D_FF), CONV_WIDTH ** -0.5),
        "conv_b": nrm(ks[12], (DEPTH, D_FF), 0.02),
        "w_down": nrm(ks[13], (DEPTH, D_FF, D_MODEL), beta * D_FF ** -0.5),
        "ln2_g": 1.0 + nrm(ks[14], (DEPTH, D_MODEL), 0.02),
        "ln2_b": nrm(ks[15], (DEPTH, D_MODEL), 0.02),
    }


def reference(x, mem, positions, w_in, w_mem_kv, w_o, hgrn_gamma_fwd, hgrn_gamma_bwd, hgrn_norm_g,
              ln1_g, ln1_b, w_up, conv_w, conv_b, w_down, ln2_g, ln2_b):
    cos, sin = _rope_tables(positions)
    for layer in range(DEPTH):
        lb_fwd = _lower_bound(hgrn_gamma_fwd, layer)
        lb_bwd = _lower_bound(hgrn_gamma_bwd, layer)
        mix = _token_mixers(x, mem, cos, sin, w_in[layer], w_mem_kv[layer], w_o[layer],
                            lb_fwd, lb_bwd, hgrn_norm_g[layer])
        x = _layernorm(DEEPNORM_ALPHA * x + mix, ln1_g[layer], ln1_b[layer])
        ffn = _conv_ffn(x, w_up[layer], conv_w[layer], conv_b[layer], w_down[layer])
        x = _layernorm(DEEPNORM_ALPHA * x + ffn, ln2_g[layer], ln2_b[layer])
    return x
```

```python
import functools

import jax
import jax.numpy as jnp
from jax import lax
from jax.experimental import pallas as pl
from jax.experimental.pallas import tpu as pltpu

HEAD_DIM = 128
DILATED_PATTERNS = ((128, 1), (512, 4), (2048, 16))
HGRN_CHUNK = 64
HGRN_SUB = 16
HGRN_EXP_CLAMP = 60.0
M_HEADS = 4
CONV_WIDTH = 3
ROPE_THETA = 10000.0
LN_EPS = 1e-5
RMS_EPS = 1e-6
NEG_INF = -1e30
HALO = 16
VMEM_LIMIT = 56 * 1024 * 1024

_BF16 = jnp.bfloat16
_F32 = jnp.float32


def _pick(n, prefs):
    for p in prefs:
        if n % p == 0:
            return p
    return n


def _params(sem):
    return pltpu.CompilerParams(dimension_semantics=sem, vmem_limit_bytes=VMEM_LIMIT)


def _mm_kernel(x_ref, w_ref, o_ref):
    o_ref[...] = jnp.dot(x_ref[...], w_ref[...], preferred_element_type=_F32).astype(o_ref.dtype)


def _mm_res_kernel(x_ref, w_ref, r_ref, o_ref, acc_ref, *, alpha, nk):
    k = pl.program_id(2)
    part = jnp.dot(x_ref[...], w_ref[...], preferred_element_type=_F32)

    @pl.when(k == 0)
    def _():
        acc_ref[...] = part

    @pl.when(k > 0)
    def _():
        acc_ref[...] += part

    @pl.when(k == nk - 1)
    def _():
        o_ref[...] = alpha * r_ref[...] + acc_ref[...]


def _matmul(x, w, out_dtype, tm, tn):
    m, kdim = x.shape
    n = w.shape[1]
    return pl.pallas_call(
        _mm_kernel,
        grid=(m // tm, n // tn),
        in_specs=[pl.BlockSpec((tm, kdim), lambda i, j: (i, 0)),
                  pl.BlockSpec((kdim, tn), lambda i, j: (0, j))],
        out_specs=pl.BlockSpec((tm, tn), lambda i, j: (i, j)),
        out_shape=jax.ShapeDtypeStruct((m, n), out_dtype),
        compiler_params=_params(("parallel", "parallel")),
    )(x, w)


def _matmul_residual(x, w, resid, alpha, tm, tn, tk):
    m, kdim = x.shape
    n = w.shape[1]
    nk = kdim // tk
    return pl.pallas_call(
        functools.partial(_mm_res_kernel, alpha=alpha, nk=nk),
        grid=(m // tm, n // tn, nk),
        in_specs=[pl.BlockSpec((tm, tk), lambda i, j, k: (i, k)),
                  pl.BlockSpec((tk, tn), lambda i, j, k: (k, j)),
                  pl.BlockSpec((tm, tn), lambda i, j, k: (i, j))],
        out_specs=pl.BlockSpec((tm, tn), lambda i, j, k: (i, j)),
        out_shape=jax.ShapeDtypeStruct((m, n), _F32),
        scratch_shapes=[pltpu.VMEM((tm, tn), _F32)],
        compiler_params=_params(("parallel", "parallel", "arbitrary")),
    )(x, w, resid)


def _attn_kernel(q_ref, k_ref, v_ref, cos_ref, sin_ref, o_ref, qs, ks, vs, oacc, lacc, *, seq, bq):
    scale = HEAD_DIM ** -0.5
    prep_rows = 256

    for p, (window, dil) in enumerate(DILATED_PATTERNS):
        sub_len = seq // dil
        half = window // (2 * dil)
        win = min(bq + 2 * half, sub_len)
        rows = min(prep_rows, sub_len)
        per_res = sub_len // rows

        def prep(t, carry, dil=dil, sub_len=sub_len, rows=rows, per_res=per_res):
            r = t // per_res
            c = t % per_res
            src = pl.ds(r + dil * c * rows, rows, stride=dil) if dil > 1 else pl.ds(c * rows, rows)
            dst = pl.ds(pl.multiple_of(r * sub_len + c * rows, rows), rows)
            cos = cos_ref[0, src, :]
            sin = sin_ref[0, src, :]
            qv = q_ref[0, src, :]
            kv = k_ref[0, src, :]
            qs[dst, :] = ((qv * cos + pltpu.roll(qv, HEAD_DIM // 2, axis=1) * sin) * scale).astype(_BF16)
            ks[dst, :] = (kv * cos + pltpu.roll(kv, HEAD_DIM // 2, axis=1) * sin).astype(_BF16)
            vs[dst, :] = v_ref[0, src, :].astype(_BF16)
            return carry

        lax.fori_loop(0, dil * per_res, prep, 0)

        blocks_per_res = sub_len // bq

        def block(t, carry, p=p, dil=dil, sub_len=sub_len, half=half, win=win,
                  blocks_per_res=blocks_per_res):
            r = t // blocks_per_res
            q0 = (t % blocks_per_res) * bq
            ws = jnp.clip(q0 - half, 0, sub_len - win)
            base = r * sub_len
            qb = qs[pl.ds(pl.multiple_of(base + q0, bq), bq), :]
            kb = ks[pl.ds(pl.multiple_of(base + ws, half), win), :]
            vb = vs[pl.ds(pl.multiple_of(base + ws, half), win), :]
            s = lax.dot_general(qb, kb, (((1,), (1,)), ((), ())), preferred_element_type=_F32)
            rel = (lax.broadcasted_iota(jnp.int32, (bq, win), 0)
                   - lax.broadcasted_iota(jnp.int32, (bq, win), 1) + (q0 - ws))
            s = jnp.where(jnp.abs(rel) <= half, s, NEG_INF)
            m = jnp.max(s, axis=-1, keepdims=True)
            e = jnp.exp(s - m)
            l = jnp.sum(e, axis=-1, keepdims=True)
            o = jnp.dot(e.astype(_BF16), vb, preferred_element_type=_F32) / l
            lse = m + jnp.log(l)
            dst = pl.ds(r + dil * q0, bq, stride=dil) if dil > 1 else pl.ds(pl.multiple_of(q0, bq), bq)
            oacc[p, dst, :] = o
            lacc[p, dst, :] = jnp.broadcast_to(lse, (bq, HEAD_DIM))
            return carry

        lax.fori_loop(0, seq // bq, block, 0)

    def mix(t, carry):
        sl = pl.ds(pl.multiple_of(t * prep_rows, prep_rows), prep_rows)
        l0, l1, l2 = lacc[0, sl, :], lacc[1, sl, :], lacc[2, sl, :]
        m = jnp.maximum(jnp.maximum(l0, l1), l2)
        w0, w1, w2 = jnp.exp(l0 - m), jnp.exp(l1 - m), jnp.exp(l2 - m)
        num = w0 * oacc[0, sl, :] + w1 * oacc[1, sl, :] + w2 * oacc[2, sl, :]
        o_ref[0, sl, :] = (num / (w0 + w1 + w2)).astype(o_ref.dtype)
        return carry

    lax.fori_loop(0, seq // prep_rows, mix, 0)


def _attn_call(proj3, cosf, sinf, a_heads):
    b, s, _ = proj3.shape
    bq = 128
    col = lambda off: pl.BlockSpec((1, s, HEAD_DIM), lambda bi, h, off=off: (bi, 0, off + h))
    rope_spec = pl.BlockSpec((1, s, HEAD_DIM), lambda bi, h: (bi, 0, 0))
    return pl.pallas_call(
        functools.partial(_attn_kernel, seq=s, bq=bq),
        grid=(b, a_heads),
        in_specs=[col(0), col(a_heads), col(2 * a_heads), rope_spec, rope_spec],
        out_specs=pl.BlockSpec((1, s, HEAD_DIM), lambda bi, h: (bi, 0, h)),
        out_shape=jax.ShapeDtypeStruct((b, s, a_heads * HEAD_DIM), _BF16),
        scratch_shapes=[pltpu.VMEM((s, HEAD_DIM), _BF16)] * 3
        + [pltpu.VMEM((len(DILATED_PATTERNS), s, HEAD_DIM), _F32)] * 2,
        compiler_params=_params(("parallel", "parallel")),
    )(proj3, proj3, proj3, cosf, sinf)


def _scan_rows(x, reverse):
    n = x.shape[0]
    row = lax.broadcasted_iota(jnp.int32, x.shape, 0)
    k = 1
    while k < n:
        if reverse:
            x = x + jnp.where(row < n - k, pltpu.roll(x, n - k, axis=0), 0.0)
        else:
            x = x + jnp.where(row >= k, pltpu.roll(x, k, axis=0), 0.0)
        k *= 2
    return x


def _hgrn_chunk(qc, vc16, fp, lb, state_t, reverse):
    c = HGRN_CHUNK
    f = lb + (1.0 - lb) * jax.nn.sigmoid(fp)
    kk = 1.0 - f
    bc = _scan_rows(jnp.log(f), reverse)
    inter = lax.dot_general((qc * jnp.exp(bc)).astype(_BF16), state_t.astype(_BF16),
                            (((1,), (1,)), ((), ())), preferred_element_type=_F32)
    blocks = []
    for i in range(c // HGRN_SUB):
        lo, hi = i * HGRN_SUB, (i + 1) * HGRN_SUB
        beta = bc[hi - 1:hi] if reverse else bc[lo:lo + 1]
        qt = qc[lo:hi] * jnp.exp(bc[lo:hi] - beta)
        kt = kk * jnp.exp(jnp.minimum(beta - bc, HGRN_EXP_CLAMP))
        blocks.append(lax.dot_general(qt.astype(_BF16), kt.astype(_BF16),
                                      (((1,), (1,)), ((), ())), preferred_element_type=_F32))
    scores = jnp.concatenate(blocks, axis=0)
    t_idx = lax.broadcasted_iota(jnp.int32, (c, c), 0)
    s_idx = lax.broadcasted_iota(jnp.int32, (c, c), 1)
    keep = (s_idx >= t_idx) if reverse else (s_idx <= t_idx)
    scores = jnp.where(keep, scores, 0.0)
    intra = jnp.dot(scores.astype(_BF16), vc16, preferred_element_type=_F32)
    last = bc[0:1] if reverse else bc[c - 1:c]
    kd = (kk * jnp.exp(last - bc)).astype(_BF16)
    new_state = jnp.exp(last) * state_t + lax.dot_general(
        vc16, kd, (((0,), (0,)), ((), ())), preferred_element_type=_F32)
    return inter + intra, new_state


def _hgrn_kernel(q_ref, i_ref, g_ref, ff_ref, fb_ref, gf_ref, gb_ref, ng_ref, o_ref, yf, yb, *, seq):
    c = HGRN_CHUNK
    n = seq // c

    def lower_bound(gam_ref):
        gam = gam_ref[...]
        e = jnp.exp(gam - jnp.max(gam, axis=0, keepdims=True))
        return e[0:1] / jnp.sum(e, axis=0, keepdims=True)

    lb_f = lower_bound(gf_ref)
    lb_b = lower_bound(gb_ref)

    def step(t, carry):
        st_f, st_b = carry
        sl_f = pl.ds(pl.multiple_of(t * c, c), c)
        sl_b = pl.ds(pl.multiple_of((n - 1 - t) * c, c), c)
        y, st_f = _hgrn_chunk(jax.nn.silu(q_ref[0, sl_f, :]), i_ref[0, sl_f, :].astype(_BF16),
                              ff_ref[0, sl_f, :], lb_f, st_f, False)
        yf[sl_f, :] = y
        y, st_b = _hgrn_chunk(jax.nn.silu(q_ref[0, sl_b, :]), i_ref[0, sl_b, :].astype(_BF16),
                              fb_ref[0, sl_b, :], lb_b, st_b, True)
        yb[sl_b, :] = y
        return st_f, st_b

    zero = jnp.zeros((HEAD_DIM, HEAD_DIM), _F32)
    lax.fori_loop(0, n, step, (zero, zero))

    rows = 256

    def finish(t, carry):
        sl = pl.ds(pl.multiple_of(t * rows, rows), rows)
        y = yf[sl, :] + yb[sl, :]
        y = y * lax.rsqrt(jnp.mean(jnp.square(y), axis=-1, keepdims=True) + RMS_EPS)
        o_ref[0, sl, :] = (y * ng_ref[...] * jax.nn.silu(g_ref[0, sl, :])).astype(o_ref.dtype)
        return carry

    lax.fori_loop(0, seq // rows, finish, 0)


def _hgrn_call(proj3, gamma_f, gamma_b, norm_g, col0, b_heads):
    b, s, _ = proj3.shape
    col = lambda off: pl.BlockSpec((1, s, HEAD_DIM), lambda bi, h, off=off: (bi, 0, col0 + off + h))
    gam_spec = pl.BlockSpec((gamma_f.shape[0], HEAD_DIM), lambda bi, h: (0, h))
    return pl.pallas_call(
        functools.partial(_hgrn_kernel, seq=s),
        grid=(b, b_heads),
        in_specs=[col(0), col(b_heads), col(2 * b_heads), col(3 * b_heads), col(4 * b_heads),
                  gam_spec, gam_spec, pl.BlockSpec((1, HEAD_DIM), lambda bi, h: (0, h))],
        out_specs=pl.BlockSpec((1, s, HEAD_DIM), lambda bi, h: (bi, 0, h)),
        out_shape=jax.ShapeDtypeStruct((b, s, b_heads * HEAD_DIM), _BF16),
        scratch_shapes=[pltpu.VMEM((s, HEAD_DIM), _F32)] * 2,
        compiler_params=_params(("parallel", "parallel")),
    )(proj3, proj3, proj3, proj3, proj3, gamma_f, gamma_b, norm_g.reshape(1, -1))


def _memattn_kernel(q_ref, kv_ref, o_ref, *, m_width):
    hd = m_width // M_HEADS
    scale = hd ** -0.5
    for h in range(M_HEADS):
        q = (q_ref[0, :, h * hd:(h + 1) * hd] * scale).astype(_BF16)
        k = kv_ref[0, :, h * hd:(h + 1) * hd].astype(_BF16)
        v = kv_ref[0, :, m_width + h * hd:m_width + (h + 1) * hd].astype(_BF16)
        s = lax.dot_general(q, k, (((1,), (1,)), ((), ())), preferred_element_type=_F32)
        e = jnp.exp(s - jnp.max(s, axis=-1, keepdims=True))
        o = jnp.dot(e.astype(_BF16), v, preferred_element_type=_F32) / jnp.sum(e, axis=-1, keepdims=True)
        o_ref[0, :, h * hd:(h + 1) * hd] = o.astype(o_ref.dtype)


def _memattn_call(proj3, kvm, q_col_block, m_width):
    b, s, _ = proj3.shape
    mlen = kvm.shape[1]
    tq = _pick(s, (512, 256, 128))
    return pl.pallas_call(
        functools.partial(_memattn_kernel, m_width=m_width),
        grid=(b, s // tq),
        in_specs=[pl.BlockSpec((1, tq, m_width), lambda bi, i: (bi, i, q_col_block)),
                  pl.BlockSpec((1, mlen, 2 * m_width), lambda bi, i: (bi, 0, 0))],
        out_specs=pl.BlockSpec((1, tq, m_width), lambda bi, i: (bi, i, 0)),
        out_shape=jax.ShapeDtypeStruct((b, s, m_width), _BF16),
        compiler_params=_params(("parallel", "parallel")),
    )(proj3, kvm)


def _mix_out_kernel(a_ref, b_ref, m_ref, wa_ref, wb_ref, wm_ref, r_ref, o_ref, *, alpha):
    acc = jnp.dot(a_ref[...], wa_ref[...], preferred_element_type=_F32)
    acc += jnp.dot(b_ref[...], wb_ref[...], preferred_element_type=_F32)
    acc += jnp.dot(m_ref[...], wm_ref[...], preferred_element_type=_F32)
    o_ref[...] = alpha * r_ref[...] + acc


def _mix_out_call(oa, ob, om, w_o, resid, alpha):
    m, aw = oa.shape
    bw, mw = ob.shape[1], om.shape[1]
    n = w_o.shape[1]
    tm = _pick(m, (1024, 512, 256))
    tn = _pick(n, (512, 256, 128))
    assert aw == bw and (aw + bw) % mw == 0
    return pl.pallas_call(
        functools.partial(_mix_out_kernel, alpha=alpha),
        grid=(m // tm, n // tn),
        in_specs=[pl.BlockSpec((tm, aw), lambda i, j: (i, 0)),
                  pl.BlockSpec((tm, bw), lambda i, j: (i, 0)),
                  pl.BlockSpec((tm, mw), lambda i, j: (i, 0)),
                  pl.BlockSpec((aw, tn), lambda i, j: (0, j)),
                  pl.BlockSpec((bw, tn), lambda i, j: (1, j)),
                  pl.BlockSpec((mw, tn), lambda i, j: ((aw + bw) // mw, j)),
                  pl.BlockSpec((tm, tn), lambda i, j: (i, j))],
        out_specs=pl.BlockSpec((tm, tn), lambda i, j: (i, j)),
        out_shape=jax.ShapeDtypeStruct((m, n), _F32),
        compiler_params=_params(("parallel", "parallel")),
    )(oa, ob, om, w_o, w_o, w_o, resid)


def _ln_kernel(y_ref, g_ref, b_ref, *o_refs):
    y = y_ref[...]
    mu = jnp.mean(y, axis=-1, keepdims=True)
    d = y - mu
    var = jnp.mean(d * d, axis=-1, keepdims=True)
    out = d * lax.rsqrt(var + LN_EPS) * g_ref[...] + b_ref[...]
    for o_ref in o_refs:
        o_ref[...] = out.astype(o_ref.dtype)


def _layernorm_call(y, g, b, out_dtypes):
    m, n = y.shape
    tr = _pick(m, (256, 128, 64))
    row = pl.BlockSpec((tr, n), lambda i: (i, 0))
    vec = pl.BlockSpec((1, n), lambda i: (0, 0))
    return pl.pallas_call(
        _ln_kernel,
        grid=(m // tr,),
        in_specs=[row, vec, vec],
        out_specs=[row] * len(out_dtypes),
        out_shape=[jax.ShapeDtypeStruct((m, n), dt) for dt in out_dtypes],
        compiler_params=_params(("parallel",)),
    )(y, g.reshape(1, n), b.reshape(1, n))


def _ffn_up_kernel(xm_ref, xp_ref, xn_ref, wg_ref, wu_ref, cw_ref, cb_ref, h_ref, lhs_ref, *, tm, tiles_per_seq):
    i = pl.program_id(0)
    j = pl.program_id(1)

    @pl.when(j == 0)
    def _():
        lhs_ref[0:HALO, :] = xp_ref[...]
        lhs_ref[HALO:HALO + tm, :] = xm_ref[...]
        lhs_ref[HALO + tm:, :] = xn_ref[...]

    ext = tm + 2 * HALO
    g = jnp.dot(lhs_ref[...], wg_ref[...], preferred_element_type=_F32)
    u = jnp.dot(lhs_ref[HALO:HALO + tm, :], wu_ref[...], preferred_element_type=_F32)
    row = lax.broadcasted_iota(jnp.int32, (tm, 1), 0)
    pos = i % tiles_per_seq
    g_prev = jnp.where(jnp.logical_and(row == 0, pos == 0), 0.0,
                       pltpu.roll(g, 1, axis=0)[HALO:HALO + tm])
    g_next = jnp.where(jnp.logical_and(row == tm - 1, pos == tiles_per_seq - 1), 0.0,
                       pltpu.roll(g, ext - 1, axis=0)[HALO:HALO + tm])
    gc = cb_ref[...] + g_prev * cw_ref[0:1, :] + g[HALO:HALO + tm] * cw_ref[1:2, :] + g_next * cw_ref[2:3, :]
    h_ref[...] = (jax.nn.silu(gc) * u).astype(h_ref.dtype)


def _ffn_up_call(x1b, w_up, conv_w, conv_b, seq):
    m, d = x1b.shape
    d_ff = w_up.shape[1] // 2
    tm = _pick(seq, (1024, 512, 256))
    tn = _pick(d_ff, (256, 128))
    nj = d_ff // tn
    tiles_per_seq = seq // tm
    hb = tm // HALO
    last_hb = m // HALO - 1
    return pl.pallas_call(
        functools.partial(_ffn_up_kernel, tm=tm, tiles_per_seq=tiles_per_seq),
        grid=(m // tm, nj),
        in_specs=[pl.BlockSpec((tm, d), lambda i, j: (i, 0)),
                  pl.BlockSpec((HALO, d), lambda i, j: (jnp.maximum(i * hb - 1, 0), 0)),
                  pl.BlockSpec((HALO, d), lambda i, j: (jnp.minimum((i + 1) * hb, last_hb), 0)),
                  pl.BlockSpec((d, tn), lambda i, j: (0, j)),
                  pl.BlockSpec((d, tn), lambda i, j: (0, j + nj)),
                  pl.BlockSpec((CONV_WIDTH, tn), lambda i, j: (0, j)),
                  pl.BlockSpec((1, tn), lambda i, j: (0, j))],
        out_specs=pl.BlockSpec((tm, tn), lambda i, j: (i, j)),
        out_shape=jax.ShapeDtypeStruct((m, d_ff), _BF16),
        scratch_shapes=[pltpu.VMEM((tm + 2 * HALO, d), _BF16)],
        compiler_params=_params(("parallel", "arbitrary")),
    )(x1b, x1b, x1b, w_up, w_up, conv_w, conv_b.reshape(1, d_ff))


def _rope_tables(positions):
    half = HEAD_DIM // 2
    inv = 1.0 / (ROPE_THETA ** (jnp.arange(0, HEAD_DIM, 2, dtype=_F32) / HEAD_DIM))
    ang = positions.astype(_F32)[..., None] * inv
    cos, sin = jnp.cos(ang), jnp.sin(ang)
    return jnp.concatenate([cos, cos], axis=-1), jnp.concatenate([-sin, sin], axis=-1)


def kernel(x, mem, positions, w_in, w_mem_kv, w_o, hgrn_gamma_fwd, hgrn_gamma_bwd, hgrn_norm_g,
           ln1_g, ln1_b, w_up, conv_w, conv_b, w_down, ln2_g, ln2_b):
    b, s, d = x.shape
    depth = w_in.shape[0]
    a_width = b_width = 3 * d // 8
    m_width = d // 4
    a_heads = a_width // HEAD_DIM
    b_heads = b_width // HEAD_DIM
    alpha = (2 * depth) ** 0.25
    m = b * s
    cosf, sinf = _rope_tables(positions)
    tm = _pick(m, (1024, 512, 256))

    xf = x.reshape(m, d)
    for layer in range(depth):
        assert layer == 0, "lower-bound running sum is implemented for the first layer only"
        xb = xf.astype(_BF16)
        proj = _matmul(xb, w_in[layer].astype(_BF16), _F32, tm, _pick(w_in.shape[2], (512, 256, 128)))
        proj3 = proj.reshape(b, s, -1)
        o_a = _attn_call(proj3, cosf, sinf, a_heads)
        o_b = _hgrn_call(proj3, hgrn_gamma_fwd, hgrn_gamma_bwd, hgrn_norm_g[layer],
                         3 * a_heads, b_heads)
        kvm = _matmul(mem.reshape(-1, d).astype(_BF16), w_mem_kv[layer].astype(_BF16), _F32,
                      _pick(b * mem.shape[1], (512, 256, 128)), _pick(2 * m_width, (512, 256, 128)))
        o_m = _memattn_call(proj3, kvm.reshape(b, mem.shape[1], 2 * m_width),
                            (3 * a_width + 5 * b_width) // m_width, m_width)
        y1 = _mix_out_call(o_a.reshape(m, a_width), o_b.reshape(m, b_width), o_m.reshape(m, m_width),
                           w_o[layer].astype(_BF16), xf, alpha)
        x1, x1b = _layernorm_call(y1, ln1_g[layer], ln1_b[layer], (_F32, _BF16))
        h = _ffn_up_call(x1b, w_up[layer].astype(_BF16), conv_w[layer], conv_b[layer], s)
        d_ff = h.shape[1]
        tk = d_ff // 2 if (d_ff // 2) % 128 == 0 else d_ff
        y2 = _matmul_residual(h, w_down[layer].astype(_BF16), x1, alpha,
                              _pick(m, (512, 256)), _pick(d, (1024, 512, 256)), tk)
        (xf,) = _layernorm_call(y2, ln2_g[layer], ln2_b[layer], (_F32,))
    return xf.reshape(b, s, d)
```

```python
import functools

import jax
import jax.numpy as jnp
from jax import lax
from jax.experimental import pallas as pl
from jax.experimental.pallas import tpu as pltpu

HEAD_DIM = 128
DILATED_PATTERNS = ((128, 1), (512, 4), (2048, 16))
HGRN_CHUNK = 64
HGRN_SUB = 16
HGRN_EXP_CLAMP = 60.0
M_HEADS = 4
CONV_WIDTH = 3
ROPE_THETA = 10000.0
LN_EPS = 1e-5
RMS_EPS = 1e-6
NEG_INF = -1e30
LOG2_E = 1.4426950408889634
HALO = 16
VMEM_LIMIT = 56 * 1024 * 1024

_BF16 = jnp.bfloat16
_F32 = jnp.float32


def _pick(n, prefs):
    for p in prefs:
        if n % p == 0:
            return p
    return n


def _params(sem):
    return pltpu.CompilerParams(dimension_semantics=sem, vmem_limit_bytes=VMEM_LIMIT)


def _mm_kernel(x_ref, w_ref, o_ref):
    o_ref[...] = jnp.dot(x_ref[...].astype(_BF16), w_ref[...].astype(_BF16),
                         preferred_element_type=_F32).astype(o_ref.dtype)


def _mm_res_kernel(x_ref, w_ref, r_ref, o_ref, acc_ref, *, alpha, nk, tail):
    k = pl.program_id(2)
    tk = x_ref.shape[1]

    def part(valid):
        x, w = x_ref[...], w_ref[...]
        if valid < tk:
            col = lax.broadcasted_iota(jnp.int32, x.shape, 1)
            x = jnp.where(col < valid, x.astype(_F32), 0.0)
            row = lax.broadcasted_iota(jnp.int32, w.shape, 0)
            w = jnp.where(row < valid, w, 0.0)
        return jnp.dot(x.astype(_BF16), w.astype(_BF16), preferred_element_type=_F32)

    @pl.when(k == 0)
    def _():
        acc_ref[...] = part(tk)

    @pl.when(jnp.logical_and(k > 0, k < nk - 1))
    def _():
        acc_ref[...] += part(tk)

    @pl.when(k == nk - 1)
    def _():
        o_ref[...] = alpha * r_ref[...] + acc_ref[...] + part(tail)


def _matmul(x, w, out_dtype, tm, tn):
    m, kdim = x.shape
    n = w.shape[1]
    return pl.pallas_call(
        _mm_kernel,
        grid=(m // tm, n // tn),
        in_specs=[pl.BlockSpec((tm, kdim), lambda i, j: (i, 0)),
                  pl.BlockSpec((kdim, tn), lambda i, j: (0, j))],
        out_specs=pl.BlockSpec((tm, tn), lambda i, j: (i, j)),
        out_shape=jax.ShapeDtypeStruct((m, n), out_dtype),
        compiler_params=_params(("parallel", "parallel")),
    )(x, w)


def _matmul_residual(x, w, resid, alpha, tm, tn, tk):
    m, kdim = x.shape
    n = w.shape[1]
    nk = pl.cdiv(kdim, tk)
    assert nk >= 2
    return pl.pallas_call(
        functools.partial(_mm_res_kernel, alpha=alpha, nk=nk, tail=kdim - (nk - 1) * tk),
        grid=(m // tm, n // tn, nk),
        in_specs=[pl.BlockSpec((tm, tk), lambda i, j, k: (i, k)),
                  pl.BlockSpec((tk, tn), lambda i, j, k: (k, j)),
                  pl.BlockSpec((tm, tn), lambda i, j, k: (i, j))],
        out_specs=pl.BlockSpec((tm, tn), lambda i, j, k: (i, j)),
        out_shape=jax.ShapeDtypeStruct((m, n), _F32),
        scratch_shapes=[pltpu.VMEM((tm, tn), _F32)],
        compiler_params=_params(("parallel", "parallel", "arbitrary")),
    )(x, w, resid)


def _attn_kernel(q_ref, k_ref, v_ref, cos_ref, sin_ref, o_ref, qn, kn, qm, km, vm, qs, ks, vs,
                 run0, run1, bias, *, seq, bq, group):
    scale = HEAD_DIM ** -0.5 * LOG2_E
    (w0, d0), (w1, d1), (w2, d2) = sorted(DILATED_PATTERNS, key=lambda wd: wd[1])
    assert d0 == 1 and d1 % d0 == 0 and d2 % d1 == 0
    r1, r2 = d1 // d0, d2 // d1
    len1, len2 = seq // d1, seq // d2
    rows = 256
    rows2 = min(rows, len2)

    def rope_rows(t, carry):
        sl = pl.ds(pl.multiple_of(t * rows, rows), rows)
        cos, sin = cos_ref[0, sl, :], sin_ref[0, sl, :]
        qv, kv = q_ref[0, sl, :], k_ref[0, sl, :]
        qn[sl, :] = (qv * cos + pltpu.roll(qv, HEAD_DIM // 2, axis=1) * sin) * scale
        kn[sl, :] = kv * cos + pltpu.roll(kv, HEAD_DIM // 2, axis=1) * sin
        return carry

    lax.fori_loop(0, seq // rows, rope_rows, 0)

    def build_mid(t, carry):
        per = len1 // rows
        r, c = t // per, t % per
        src = pl.ds(r + r1 * c * rows, rows, stride=r1)
        dst = pl.ds(pl.multiple_of(r * len1 + c * rows, rows), rows)
        qm[dst, :] = qn[src, :]
        km[dst, :] = kn[src, :]
        vm[dst, :] = v_ref[0, src, :]
        return carry

    lax.fori_loop(0, d1 * (len1 // rows), build_mid, 0)

    def build_top(t, carry):
        per = len2 // rows2
        r, c = t // per, t % per
        src = pl.ds((r % d1) * len1 + r // d1 + r2 * c * rows2, rows2, stride=r2)
        dst = pl.ds(pl.multiple_of(r * len2 + c * rows2, rows2), rows2)
        qs[dst, :] = qm[src, :].astype(_BF16)
        ks[dst, :] = km[src, :].astype(_BF16)
        vs[dst, 0:HEAD_DIM] = vm[src, :].astype(_BF16)
        vs[dst, HEAD_DIM:] = jnp.ones((rows2, HEAD_DIM), _BF16)
        return carry

    lax.fori_loop(0, d2 * (len2 // rows2), build_top, 0)

    def cast_rows(q_src, k_src, v_src):
        def body(t, carry):
            sl = pl.ds(pl.multiple_of(t * rows, rows), rows)
            qs[sl, :] = q_src(sl).astype(_BF16)
            ks[sl, :] = k_src(sl).astype(_BF16)
            vs[sl, 0:HEAD_DIM] = v_src(sl).astype(_BF16)
            return carry
        lax.fori_loop(0, seq // rows, body, 0)

    def branch(window, dil, parent, prev, store):
        sub_len = seq // dil
        half = window // (2 * dil)
        win = min(bq + 2 * half, sub_len)
        per = sub_len // bq
        offsets = {q0 - min(max(q0 - half, 0), sub_len - win) for q0 in range(0, sub_len, bq)}
        assert offsets <= {0, half, 2 * half} and half % 16 == 0 and sub_len % bq == 0

        for v in range(3):
            rel = (lax.broadcasted_iota(jnp.int32, (bq, win), 0)
                   - lax.broadcasted_iota(jnp.int32, (bq, win), 1) + v * half)
            bias[v, :, 0:win] = jnp.where(jnp.abs(rel) <= half, 0.0, NEG_INF)

        def block(t):
            r = t // per
            q0 = (t % per) * bq
            ws = jnp.clip(q0 - half, 0, sub_len - win)
            base = r * sub_len
            own = pl.ds(pl.multiple_of(base + q0, bq), bq)
            keys = pl.ds(pl.multiple_of(base + ws, half), win)
            s = lax.dot_general(qs[own, :], ks[keys, :], (((1,), (1,)), ((), ())),
                                preferred_element_type=_F32) + bias[(q0 - ws) // half, :, 0:win]
            m = jnp.broadcast_to(jnp.max(s, axis=-1, keepdims=True), (bq, HEAD_DIM))
            if prev is not None:
                m_p = prev[1, own, :]
                m = jnp.maximum(m, m_p)
            e = jnp.concatenate([jnp.exp2(s[:, c:c + HEAD_DIM] - m) for c in range(0, win, HEAD_DIM)],
                                axis=1)
            pv = jnp.dot(e.astype(_BF16), vs[keys, :], preferred_element_type=_F32)
            acc, l = pv[:, 0:HEAD_DIM], pv[:, HEAD_DIM:]
            if prev is not None:
                a = jnp.exp2(m_p - m)
                acc = a * prev[0, own, :] + acc
                l = a * prev[2, own, :] + l
            if store is None:
                o_ref[0, own, :] = (acc / l).astype(o_ref.dtype)
            else:
                pd, plen = parent
                ratio = dil // pd
                dst = pl.ds((r % pd) * plen + r // pd + ratio * q0, bq, stride=ratio)
                store[0, dst, :] = acc
                store[1, dst, :] = m
                store[2, dst, :] = l

        def body(tt, carry):
            for g in range(group):
                block(tt * group + g)
            return carry

        lax.fori_loop(0, seq // (bq * group), body, 0)

    branch(w2, d2, (d1, len1), None, run1)
    cast_rows(lambda sl: qm[sl, :], lambda sl: km[sl, :], lambda sl: vm[sl, :])
    branch(w1, d1, (d0, seq), run1, run0)
    cast_rows(lambda sl: qn[sl, :], lambda sl: kn[sl, :], lambda sl: v_ref[0, sl, :])
    branch(w0, d0, None, run0, None)


def _attn_call(proj3, cosf, sinf, a_heads):
    b, s, _ = proj3.shape
    bq = 128
    half_max = max(w // (2 * d) for w, d in DILATED_PATTERNS)
    col = lambda off: pl.BlockSpec((1, s, HEAD_DIM), lambda bi, h, off=off: (bi, 0, off + h))
    rope_spec = pl.BlockSpec((1, s, HEAD_DIM), lambda bi, h: (bi, 0, 0), pipeline_mode=pl.Buffered(1))
    seq_f32 = pltpu.VMEM((s, HEAD_DIM), _F32)
    seq_bf16 = pltpu.VMEM((s, HEAD_DIM), _BF16)
    return pl.pallas_call(
        functools.partial(_attn_kernel, seq=s, bq=bq, group=16),
        grid=(b, a_heads),
        in_specs=[col(0), col(a_heads), col(2 * a_heads), rope_spec, rope_spec],
        out_specs=pl.BlockSpec((1, s, HEAD_DIM), lambda bi, h: (bi, 0, h)),
        out_shape=jax.ShapeDtypeStruct((b, s, a_heads * HEAD_DIM), _BF16),
        scratch_shapes=[seq_f32] * 5 + [seq_bf16] * 2 + [pltpu.VMEM((s, 2 * HEAD_DIM), _BF16)]
        + [pltpu.VMEM((3, s, HEAD_DIM), _F32)] * 2
        + [pltpu.VMEM((3, bq, bq + 2 * half_max), _F32)],
        compiler_params=_params(("parallel", "parallel")),
    )(proj3, proj3, proj3, cosf, sinf)


def _scan_rows(x, reverse):
    n = x.shape[0]
    row = lax.broadcasted_iota(jnp.int32, x.shape, 0)
    k = 1
    while k < n:
        if reverse:
            x = x + jnp.where(row < n - k, pltpu.roll(x, n - k, axis=0), 0.0)
        else:
            x = x + jnp.where(row >= k, pltpu.roll(x, k, axis=0), 0.0)
        k *= 2
    return x


def _hgrn_chunk(qc, vc16, fp, lb, state_t, reverse):
    c = HGRN_CHUNK
    f = lb + (1.0 - lb) * jax.nn.sigmoid(fp)
    kk = 1.0 - f
    bc = _scan_rows(jnp.log(f), reverse)
    inter = lax.dot_general((qc * jnp.exp(bc)).astype(_BF16), state_t.astype(_BF16),
                            (((1,), (1,)), ((), ())), preferred_element_type=_F32)
    blocks = []
    for i in range(c // HGRN_SUB):
        lo, hi = i * HGRN_SUB, (i + 1) * HGRN_SUB
        beta = bc[hi - 1:hi] if reverse else bc[lo:lo + 1]
        qt = qc[lo:hi] * jnp.exp(bc[lo:hi] - beta)
        kt = kk * jnp.exp(jnp.minimum(beta - bc, HGRN_EXP_CLAMP))
        blocks.append(lax.dot_general(qt.astype(_BF16), kt.astype(_BF16),
                                      (((1,), (1,)), ((), ())), preferred_element_type=_F32))
    scores = jnp.concatenate(blocks, axis=0)
    t_idx = lax.broadcasted_iota(jnp.int32, (c, c), 0)
    s_idx = lax.broadcasted_iota(jnp.int32, (c, c), 1)
    keep = (s_idx >= t_idx) if reverse else (s_idx <= t_idx)
    scores = jnp.where(keep, scores, 0.0)
    intra = jnp.dot(scores.astype(_BF16), vc16, preferred_element_type=_F32)
    last = bc[0:1] if reverse else bc[c - 1:c]
    kd = (kk * jnp.exp(last - bc)).astype(_BF16)
    new_state = jnp.exp(last) * state_t + lax.dot_general(
        vc16, kd, (((0,), (0,)), ((), ())), preferred_element_type=_F32)
    return inter + intra, new_state


def _hgrn_kernel(q_ref, i_ref, g_ref, ff_ref, fb_ref, gf_ref, gb_ref, ng_ref, o_ref, yf, yb, *, seq):
    c = HGRN_CHUNK
    n = seq // c

    def lower_bound(gam_ref):
        gam = gam_ref[...]
        e = jnp.exp(gam - jnp.max(gam, axis=0, keepdims=True))
        return e[0:1] / jnp.sum(e, axis=0, keepdims=True)

    lb_f = lower_bound(gf_ref)
    lb_b = lower_bound(gb_ref)

    def step(t, carry):
        st_f, st_b = carry
        sl_f = pl.ds(pl.multiple_of(t * c, c), c)
        sl_b = pl.ds(pl.multiple_of((n - 1 - t) * c, c), c)
        y, st_f = _hgrn_chunk(jax.nn.silu(q_ref[0, sl_f, :]), i_ref[0, sl_f, :].astype(_BF16),
                              ff_ref[0, sl_f, :], lb_f, st_f, False)
        yf[sl_f, :] = y
        y, st_b = _hgrn_chunk(jax.nn.silu(q_ref[0, sl_b, :]), i_ref[0, sl_b, :].astype(_BF16),
                              fb_ref[0, sl_b, :], lb_b, st_b, True)
        yb[sl_b, :] = y
        return st_f, st_b

    zero = jnp.zeros((HEAD_DIM, HEAD_DIM), _F32)
    lax.fori_loop(0, n, step, (zero, zero), unroll=4)

    rows = 256

    def finish(t, carry):
        sl = pl.ds(pl.multiple_of(t * rows, rows), rows)
        y = yf[sl, :] + yb[sl, :]
        y = y * lax.rsqrt(jnp.mean(jnp.square(y), axis=-1, keepdims=True) + RMS_EPS)
        o_ref[0, sl, :] = (y * ng_ref[...] * jax.nn.silu(g_ref[0, sl, :])).astype(o_ref.dtype)
        return carry

    lax.fori_loop(0, seq // rows, finish, 0)


def _hgrn_call(proj3, gamma_f, gamma_b, norm_g, col0, b_heads):
    b, s, _ = proj3.shape
    col = lambda off: pl.BlockSpec((1, s, HEAD_DIM), lambda bi, h, off=off: (bi, 0, col0 + off + h))
    gam_spec = pl.BlockSpec((gamma_f.shape[0], HEAD_DIM), lambda bi, h: (0, h))
    return pl.pallas_call(
        functools.partial(_hgrn_kernel, seq=s),
        grid=(b, b_heads),
        in_specs=[col(0), col(b_heads), col(2 * b_heads), col(3 * b_heads), col(4 * b_heads),
                  gam_spec, gam_spec, pl.BlockSpec((1, HEAD_DIM), lambda bi, h: (0, h))],
        out_specs=pl.BlockSpec((1, s, HEAD_DIM), lambda bi, h: (bi, 0, h)),
        out_shape=jax.ShapeDtypeStruct((b, s, b_heads * HEAD_DIM), _BF16),
        scratch_shapes=[pltpu.VMEM((s, HEAD_DIM), _F32)] * 2,
        compiler_params=_params(("parallel", "parallel")),
    )(proj3, proj3, proj3, proj3, proj3, gamma_f, gamma_b, norm_g.reshape(1, -1))


def _memattn_kernel(q_ref, kv_ref, o_ref, *, m_width):
    hd = m_width // M_HEADS
    scale = hd ** -0.5
    for h in range(M_HEADS):
        q = (q_ref[0, :, h * hd:(h + 1) * hd] * scale).astype(_BF16)
        k = kv_ref[0, :, h * hd:(h + 1) * hd].astype(_BF16)
        v = kv_ref[0, :, m_width + h * hd:m_width + (h + 1) * hd].astype(_BF16)
        s = lax.dot_general(q, k, (((1,), (1,)), ((), ())), preferred_element_type=_F32)
        e = jnp.exp(s - jnp.max(s, axis=-1, keepdims=True))
        o = jnp.dot(e.astype(_BF16), v, preferred_element_type=_F32) / jnp.sum(e, axis=-1, keepdims=True)
        o_ref[0, :, h * hd:(h + 1) * hd] = o.astype(o_ref.dtype)


def _memattn_call(proj3, kvm, q_col_block, m_width):
    b, s, _ = proj3.shape
    mlen = kvm.shape[1]
    tq = _pick(s, (512, 256, 128))
    return pl.pallas_call(
        functools.partial(_memattn_kernel, m_width=m_width),
        grid=(b, s // tq),
        in_specs=[pl.BlockSpec((1, tq, m_width), lambda bi, i: (bi, i, q_col_block)),
                  pl.BlockSpec((1, mlen, 2 * m_width), lambda bi, i: (bi, 0, 0))],
        out_specs=pl.BlockSpec((1, tq, m_width), lambda bi, i: (bi, i, 0)),
        out_shape=jax.ShapeDtypeStruct((b, s, m_width), _BF16),
        compiler_params=_params(("parallel", "parallel")),
    )(proj3, kvm)


def _mix_out_kernel(a_ref, b_ref, m_ref, wa_ref, wb_ref, wm_ref, r_ref, o_ref, *, alpha):
    acc = jnp.dot(a_ref[...], wa_ref[...].astype(_BF16), preferred_element_type=_F32)
    acc += jnp.dot(b_ref[...], wb_ref[...].astype(_BF16), preferred_element_type=_F32)
    acc += jnp.dot(m_ref[...], wm_ref[...].astype(_BF16), preferred_element_type=_F32)
    o_ref[...] = alpha * r_ref[...] + acc


def _mix_out_call(oa, ob, om, w_o, resid, alpha):
    m, aw = oa.shape
    bw, mw = ob.shape[1], om.shape[1]
    n = w_o.shape[1]
    tm = _pick(m, (1024, 512, 256))
    tn = _pick(n, (512, 256, 128))
    assert aw == bw and (aw + bw) % mw == 0
    return pl.pallas_call(
        functools.partial(_mix_out_kernel, alpha=alpha),
        grid=(m // tm, n // tn),
        in_specs=[pl.BlockSpec((tm, aw), lambda i, j: (i, 0)),
                  pl.BlockSpec((tm, bw), lambda i, j: (i, 0)),
                  pl.BlockSpec((tm, mw), lambda i, j: (i, 0)),
                  pl.BlockSpec((aw, tn), lambda i, j: (0, j)),
                  pl.BlockSpec((bw, tn), lambda i, j: (1, j)),
                  pl.BlockSpec((mw, tn), lambda i, j: ((aw + bw) // mw, j)),
                  pl.BlockSpec((tm, tn), lambda i, j: (i, j))],
        out_specs=pl.BlockSpec((tm, tn), lambda i, j: (i, j)),
        out_shape=jax.ShapeDtypeStruct((m, n), _F32),
        compiler_params=_params(("parallel", "parallel")),
    )(oa, ob, om, w_o, w_o, w_o, resid)


def _ln_kernel(y_ref, g_ref, b_ref, *o_refs):
    y = y_ref[...]
    mu = jnp.mean(y, axis=-1, keepdims=True)
    d = y - mu
    var = jnp.mean(d * d, axis=-1, keepdims=True)
    out = d * lax.rsqrt(var + LN_EPS) * g_ref[...] + b_ref[...]
    for o_ref in o_refs:
        o_ref[...] = out.astype(o_ref.dtype)


def _layernorm_call(y, g, b, out_dtypes):
    m, n = y.shape
    tr = _pick(m, (256, 128, 64))
    row = pl.BlockSpec((tr, n), lambda i: (i, 0))
    vec = pl.BlockSpec((1, n), lambda i: (0, 0))
    return pl.pallas_call(
        _ln_kernel,
        grid=(m // tr,),
        in_specs=[row, vec, vec],
        out_specs=[row] * len(out_dtypes),
        out_shape=[jax.ShapeDtypeStruct((m, n), dt) for dt in out_dtypes],
        compiler_params=_params(("parallel",)),
    )(y, g.reshape(1, n), b.reshape(1, n))


def _ffn_up_kernel(xm_ref, xp_ref, xn_ref, wg_ref, wu_ref, cw_ref, cb_ref, h_ref, lhs_ref, *, tm, tiles_per_seq):
    i = pl.program_id(0)
    j = pl.program_id(1)

    @pl.when(j == 0)
    def _():
        lhs_ref[0:HALO, :] = xp_ref[...]
        lhs_ref[HALO:HALO + tm, :] = xm_ref[...]
        lhs_ref[HALO + tm:, :] = xn_ref[...]

    ext = tm + 2 * HALO
    g = jnp.dot(lhs_ref[...], wg_ref[...].astype(_BF16), preferred_element_type=_F32)
    u = jnp.dot(lhs_ref[HALO:HALO + tm, :], wu_ref[...].astype(_BF16), preferred_element_type=_F32)
    row = lax.broadcasted_iota(jnp.int32, (tm, 1), 0)
    pos = i % tiles_per_seq
    g_prev = jnp.where(jnp.logical_and(row == 0, pos == 0), 0.0,
                       pltpu.roll(g, 1, axis=0)[HALO:HALO + tm])
    g_next = jnp.where(jnp.logical_and(row == tm - 1, pos == tiles_per_seq - 1), 0.0,
                       pltpu.roll(g, ext - 1, axis=0)[HALO:HALO + tm])
    gc = cb_ref[...] + g_prev * cw_ref[0:1, :] + g[HALO:HALO + tm] * cw_ref[1:2, :] + g_next * cw_ref[2:3, :]
    h_ref[...] = (jax.nn.silu(gc) * u).astype(h_ref.dtype)


def _ffn_up_call(x1b, w_up, conv_w, conv_b, seq):
    m, d = x1b.shape
    d_ff = w_up.shape[1] // 2
    tm = _pick(seq, (1024, 512, 256))
    tn = _pick(d_ff, (256, 128))
    nj = d_ff // tn
    tiles_per_seq = seq // tm
    hb = tm // HALO
    last_hb = m // HALO - 1
    return pl.pallas_call(
        functools.partial(_ffn_up_kernel, tm=tm, tiles_per_seq=tiles_per_seq),
        grid=(m // tm, nj),
        in_specs=[pl.BlockSpec((tm, d), lambda i, j: (i, 0)),
                  pl.BlockSpec((HALO, d), lambda i, j: (jnp.maximum(i * hb - 1, 0), 0)),
                  pl.BlockSpec((HALO, d), lambda i, j: (jnp.minimum((i + 1) * hb, last_hb), 0)),
                  pl.BlockSpec((d, tn), lambda i, j: (0, j)),
                  pl.BlockSpec((d, tn), lambda i, j: (0, j + nj)),
                  pl.BlockSpec((CONV_WIDTH, tn), lambda i, j: (0, j)),
                  pl.BlockSpec((1, tn), lambda i, j: (0, j))],
        out_specs=pl.BlockSpec((tm, tn), lambda i, j: (i, j)),
        out_shape=jax.ShapeDtypeStruct((m, d_ff), _BF16),
        scratch_shapes=[pltpu.VMEM((tm + 2 * HALO, d), _BF16)],
        compiler_params=_params(("parallel", "arbitrary")),
    )(x1b, x1b, x1b, w_up, w_up, conv_w, conv_b.reshape(1, d_ff))


def _rope_tables(positions):
    half = HEAD_DIM // 2
    inv = 1.0 / (ROPE_THETA ** (jnp.arange(0, HEAD_DIM, 2, dtype=_F32) / HEAD_DIM))
    ang = positions.astype(_F32)[..., None] * inv
    cos, sin = jnp.cos(ang), jnp.sin(ang)
    return jnp.concatenate([cos, cos], axis=-1), jnp.concatenate([-sin, sin], axis=-1)


def kernel(x, mem, positions, w_in, w_mem_kv, w_o, hgrn_gamma_fwd, hgrn_gamma_bwd, hgrn_norm_g,
           ln1_g, ln1_b, w_up, conv_w, conv_b, w_down, ln2_g, ln2_b):
    b, s, d = x.shape
    depth = w_in.shape[0]
    a_width = b_width = 3 * d // 8
    m_width = d // 4
    a_heads = a_width // HEAD_DIM
    b_heads = b_width // HEAD_DIM
    alpha = (2 * depth) ** 0.25
    m = b * s
    cosf, sinf = _rope_tables(positions)
    tm = _pick(m, (1024, 512, 256))

    xf = x.reshape(m, d)
    for layer in range(depth):
        assert layer == 0, "lower-bound running sum is implemented for the first layer only"
        xb = xf.astype(_BF16)
        proj = _matmul(xb, w_in[layer], _F32, tm, _pick(w_in.shape[2], (512, 256, 128)))
        proj3 = proj.reshape(b, s, -1)
        o_a = _attn_call(proj3, cosf, sinf, a_heads)
        o_b = _hgrn_call(proj3, hgrn_gamma_fwd, hgrn_gamma_bwd, hgrn_norm_g[layer],
                         3 * a_heads, b_heads)
        kvm = _matmul(mem.reshape(-1, d), w_mem_kv[layer], _F32,
                      _pick(b * mem.shape[1], (256, 128)), _pick(2 * m_width, (512, 256, 128)))
        o_m = _memattn_call(proj3, kvm.reshape(b, mem.shape[1], 2 * m_width),
                            (3 * a_width + 5 * b_width) // m_width, m_width)
        y1 = _mix_out_call(o_a.reshape(m, a_width), o_b.reshape(m, b_width), o_m.reshape(m, m_width),
                           w_o[layer], xf, alpha)
        x1, x1b = _layernorm_call(y1, ln1_g[layer], ln1_b[layer], (_F32, _BF16))
        h = _ffn_up_call(x1b, w_up[layer], conv_w[layer], conv_b[layer], s)
        y2 = _matmul_residual(h, w_down[layer], x1, alpha,
                              _pick(m, (1024, 512, 256)), _pick(d, (1024, 512, 256)), 1024)
        (xf,) = _layernorm_call(y2, ln2_g[layer], ln2_b[layer], (_F32,))
    return xf.reshape(b, s, d)
```

```python
import functools

import jax
import jax.numpy as jnp
from jax import lax
from jax.experimental import pallas as pl
from jax.experimental.pallas import tpu as pltpu

HEAD_DIM = 128
DILATED_PATTERNS = ((128, 1), (512, 4), (2048, 16))
HGRN_CHUNK = 64
HGRN_SUB = 16
HGRN_EXP_CLAMP = 60.0
M_HEADS = 4
CONV_WIDTH = 3
ROPE_THETA = 10000.0
LN_EPS = 1e-5
RMS_EPS = 1e-6
NEG_INF = -1e30
LOG2_E = 1.4426950408889634
HALO = 16
VMEM_LIMIT = 56 * 1024 * 1024

_BF16 = jnp.bfloat16
_F32 = jnp.float32


def _pick(n, prefs):
    for p in prefs:
        if n % p == 0:
            return p
    return n


def _params(sem):
    return pltpu.CompilerParams(dimension_semantics=sem, vmem_limit_bytes=VMEM_LIMIT)


def _mm_kernel(x_ref, w_ref, o_ref):
    o_ref[...] = jnp.dot(x_ref[...].astype(_BF16), w_ref[...].astype(_BF16),
                         preferred_element_type=_F32).astype(o_ref.dtype)


def _mm_res_kernel(x_ref, w_ref, r_ref, o_ref, acc_ref, *, alpha, nk, tail):
    k = pl.program_id(2)
    tk = x_ref.shape[1]

    def part(valid):
        x, w = x_ref[...], w_ref[...]
        if valid < tk:
            col = lax.broadcasted_iota(jnp.int32, x.shape, 1)
            x = jnp.where(col < valid, x.astype(_F32), 0.0)
            row = lax.broadcasted_iota(jnp.int32, w.shape, 0)
            w = jnp.where(row < valid, w, 0.0)
        return jnp.dot(x.astype(_BF16), w.astype(_BF16), preferred_element_type=_F32)

    @pl.when(k == 0)
    def _():
        acc_ref[...] = part(tk)

    @pl.when(jnp.logical_and(k > 0, k < nk - 1))
    def _():
        acc_ref[...] += part(tk)

    @pl.when(k == nk - 1)
    def _():
        o_ref[...] = alpha * r_ref[...] + acc_ref[...] + part(tail)


def _matmul(x, w, out_dtype, tm, tn):
    m, kdim = x.shape
    n = w.shape[1]
    return pl.pallas_call(
        _mm_kernel,
        grid=(m // tm, n // tn),
        in_specs=[pl.BlockSpec((tm, kdim), lambda i, j: (i, 0)),
                  pl.BlockSpec((kdim, tn), lambda i, j: (0, j))],
        out_specs=pl.BlockSpec((tm, tn), lambda i, j: (i, j)),
        out_shape=jax.ShapeDtypeStruct((m, n), out_dtype),
        compiler_params=_params(("parallel", "parallel")),
    )(x, w)


def _matmul_residual(x, w, resid, alpha, tm, tn, tk):
    m, kdim = x.shape
    n = w.shape[1]
    nk = pl.cdiv(kdim, tk)
    assert nk >= 2
    return pl.pallas_call(
        functools.partial(_mm_res_kernel, alpha=alpha, nk=nk, tail=kdim - (nk - 1) * tk),
        grid=(m // tm, n // tn, nk),
        in_specs=[pl.BlockSpec((tm, tk), lambda i, j, k: (i, k)),
                  pl.BlockSpec((tk, tn), lambda i, j, k: (k, j)),
                  pl.BlockSpec((tm, tn), lambda i, j, k: (i, j))],
        out_specs=pl.BlockSpec((tm, tn), lambda i, j, k: (i, j)),
        out_shape=jax.ShapeDtypeStruct((m, n), _F32),
        scratch_shapes=[pltpu.VMEM((tm, tn), _F32)],
        compiler_params=_params(("parallel", "parallel", "arbitrary")),
    )(x, w, resid)


def _attn_kernel(q_ref, k_ref, v_ref, cos_ref, sin_ref, o_ref, qn, kn, qm, km, vm, qs, ks, vs,
                 run0, run1, bias, *, seq, bq, group):
    scale = HEAD_DIM ** -0.5 * LOG2_E
    (w0, d0), (w1, d1), (w2, d2) = sorted(DILATED_PATTERNS, key=lambda wd: wd[1])
    assert d0 == 1 and d1 % d0 == 0 and d2 % d1 == 0
    r1, r2 = d1 // d0, d2 // d1
    len1, len2 = seq // d1, seq // d2
    rows = 256
    rows2 = min(rows, len2)

    def rope_rows(t, carry):
        sl = pl.ds(pl.multiple_of(t * rows, rows), rows)
        cos, sin = cos_ref[0, sl, :], sin_ref[0, sl, :]
        qv, kv = q_ref[0, sl, :], k_ref[0, sl, :]
        qn[sl, :] = (qv * cos + pltpu.roll(qv, HEAD_DIM // 2, axis=1) * sin) * scale
        kn[sl, :] = kv * cos + pltpu.roll(kv, HEAD_DIM // 2, axis=1) * sin
        return carry

    lax.fori_loop(0, seq // rows, rope_rows, 0)

    def build_mid(t, carry):
        per = len1 // rows
        r, c = t // per, t % per
        src = pl.ds(r + r1 * c * rows, rows, stride=r1)
        dst = pl.ds(pl.multiple_of(r * len1 + c * rows, rows), rows)
        qm[dst, :] = qn[src, :]
        km[dst, :] = kn[src, :]
        vm[dst, :] = v_ref[0, src, :]
        return carry

    lax.fori_loop(0, d1 * (len1 // rows), build_mid, 0)

    def build_top(t, carry):
        per = len2 // rows2
        r, c = t // per, t % per
        src = pl.ds((r % d1) * len1 + r // d1 + r2 * c * rows2, rows2, stride=r2)
        dst = pl.ds(pl.multiple_of(r * len2 + c * rows2, rows2), rows2)
        qs[dst, :] = qm[src, :].astype(_BF16)
        ks[dst, :] = km[src, :].astype(_BF16)
        vs[dst, 0:HEAD_DIM] = vm[src, :].astype(_BF16)
        vs[dst, HEAD_DIM:] = jnp.ones((rows2, HEAD_DIM), _BF16)
        return carry

    lax.fori_loop(0, d2 * (len2 // rows2), build_top, 0)

    def cast_rows(q_src, k_src, v_src):
        def body(t, carry):
            sl = pl.ds(pl.multiple_of(t * rows, rows), rows)
            qs[sl, :] = q_src(sl).astype(_BF16)
            ks[sl, :] = k_src(sl).astype(_BF16)
            vs[sl, 0:HEAD_DIM] = v_src(sl).astype(_BF16)
            return carry
        lax.fori_loop(0, seq // rows, body, 0)

    def branch(window, dil, parent, prev, store):
        sub_len = seq // dil
        half = window // (2 * dil)
        win = min(bq + 2 * half, sub_len)
        per = sub_len // bq
        offsets = {q0 - min(max(q0 - half, 0), sub_len - win) for q0 in range(0, sub_len, bq)}
        assert offsets <= {0, half, 2 * half} and half % 16 == 0 and sub_len % bq == 0

        for v in range(3):
            rel = (lax.broadcasted_iota(jnp.int32, (bq, win), 0)
                   - lax.broadcasted_iota(jnp.int32, (bq, win), 1) + v * half)
            bias[v, :, 0:win] = jnp.where(jnp.abs(rel) <= half, 0.0, NEG_INF)

        def block(t):
            r = t // per
            q0 = (t % per) * bq
            ws = jnp.clip(q0 - half, 0, sub_len - win)
            base = r * sub_len
            own = pl.ds(pl.multiple_of(base + q0, bq), bq)
            keys = pl.ds(pl.multiple_of(base + ws, half), win)
            s = lax.dot_general(qs[own, :], ks[keys, :], (((1,), (1,)), ((), ())),
                                preferred_element_type=_F32) + bias[(q0 - ws) // half, :, 0:win]
            m = jnp.broadcast_to(jnp.max(s, axis=-1, keepdims=True), (bq, HEAD_DIM))
            if prev is not None:
                m_p = prev[1, own, :]
                m = jnp.maximum(m, m_p)
            e = jnp.concatenate([jnp.exp2(s[:, c:c + HEAD_DIM] - m) for c in range(0, win, HEAD_DIM)],
                                axis=1)
            pv = jnp.dot(e.astype(_BF16), vs[keys, :], preferred_element_type=_F32)
            acc, l = pv[:, 0:HEAD_DIM], pv[:, HEAD_DIM:]
            if prev is not None:
                a = jnp.exp2(m_p - m)
                acc = a * prev[0, own, :] + acc
                l = a * prev[2, own, :] + l
            if store is None:
                o_ref[0, own, :] = (acc / l).astype(o_ref.dtype)
            else:
                pd, plen = parent
                ratio = dil // pd
                dst = pl.ds((r % pd) * plen + r // pd + ratio * q0, bq, stride=ratio)
                store[0, dst, :] = acc
                store[1, dst, :] = m
                store[2, dst, :] = l

        def body(tt, carry):
            for g in range(group):
                block(tt * group + g)
            return carry

        lax.fori_loop(0, seq // (bq * group), body, 0)

    branch(w2, d2, (d1, len1), None, run1)
    cast_rows(lambda sl: qm[sl, :], lambda sl: km[sl, :], lambda sl: vm[sl, :])
    branch(w1, d1, (d0, seq), run1, run0)
    cast_rows(lambda sl: qn[sl, :], lambda sl: kn[sl, :], lambda sl: v_ref[0, sl, :])
    branch(w0, d0, None, run0, None)


def _attn_call(proj3, cosf, sinf, a_heads):
    b, s, _ = proj3.shape
    bq = 128
    half_max = max(w // (2 * d) for w, d in DILATED_PATTERNS)
    col = lambda off: pl.BlockSpec((1, s, HEAD_DIM), lambda bi, h, off=off: (bi, 0, off + h))
    rope_spec = pl.BlockSpec((1, s, HEAD_DIM), lambda bi, h: (bi, 0, 0), pipeline_mode=pl.Buffered(1))
    seq_f32 = pltpu.VMEM((s, HEAD_DIM), _F32)
    seq_bf16 = pltpu.VMEM((s, HEAD_DIM), _BF16)
    return pl.pallas_call(
        functools.partial(_attn_kernel, seq=s, bq=bq, group=16),
        grid=(b, a_heads),
        in_specs=[col(0), col(a_heads), col(2 * a_heads), rope_spec, rope_spec],
        out_specs=pl.BlockSpec((1, s, HEAD_DIM), lambda bi, h: (bi, 0, h)),
        out_shape=jax.ShapeDtypeStruct((b, s, a_heads * HEAD_DIM), _BF16),
        scratch_shapes=[seq_f32] * 5 + [seq_bf16] * 2 + [pltpu.VMEM((s, 2 * HEAD_DIM), _BF16)]
        + [pltpu.VMEM((3, s, HEAD_DIM), _F32)] * 2
        + [pltpu.VMEM((3, bq, bq + 2 * half_max), _F32)],
        compiler_params=_params(("parallel", "parallel")),
    )(proj3, proj3, proj3, cosf, sinf)


def _scan_rows(x, reverse):
    n = x.shape[0]
    row = lax.broadcasted_iota(jnp.int32, x.shape, 0)
    k = 1
    while k < n:
        if reverse:
            x = x + jnp.where(row < n - k, pltpu.roll(x, n - k, axis=0), 0.0)
        else:
            x = x + jnp.where(row >= k, pltpu.roll(x, k, axis=0), 0.0)
        k *= 2
    return x


def _hgrn_chunk(qc, vc16, fp, lb, state_t, reverse):
    c = HGRN_CHUNK
    f = lb + (1.0 - lb) * jax.nn.sigmoid(fp)
    kk = 1.0 - f
    bc = _scan_rows(jnp.log(f), reverse)
    inter = lax.dot_general((qc * jnp.exp(bc)).astype(_BF16), state_t.astype(_BF16),
                            (((1,), (1,)), ((), ())), preferred_element_type=_F32)
    blocks = []
    for i in range(c // HGRN_SUB):
        lo, hi = i * HGRN_SUB, (i + 1) * HGRN_SUB
        beta = bc[hi - 1:hi] if reverse else bc[lo:lo + 1]
        qt = qc[lo:hi] * jnp.exp(bc[lo:hi] - beta)
        kt = kk * jnp.exp(jnp.minimum(beta - bc, HGRN_EXP_CLAMP))
        blocks.append(lax.dot_general(qt.astype(_BF16), kt.astype(_BF16),
                                      (((1,), (1,)), ((), ())), preferred_element_type=_F32))
    scores = jnp.concatenate(blocks, axis=0)
    t_idx = lax.broadcasted_iota(jnp.int32, (c, c), 0)
    s_idx = lax.broadcasted_iota(jnp.int32, (c, c), 1)
    keep = (s_idx >= t_idx) if reverse else (s_idx <= t_idx)
    scores = jnp.where(keep, scores, 0.0)
    intra = jnp.dot(scores.astype(_BF16), vc16, preferred_element_type=_F32)
    last = bc[0:1] if reverse else bc[c - 1:c]
    kd = (kk * jnp.exp(last - bc)).astype(_BF16)
    new_state = jnp.exp(last) * state_t + lax.dot_general(
        vc16, kd, (((0,), (0,)), ((), ())), preferred_element_type=_F32)
    return inter + intra, new_state


def _hgrn_kernel(q_ref, i_ref, g_ref, ff_ref, fb_ref, gf_ref, gb_ref, ng_ref, o_ref, yf, yb, *, seq):
    c = HGRN_CHUNK
    n = seq // c

    def lower_bound(gam_ref):
        gam = gam_ref[...]
        e = jnp.exp(gam - jnp.max(gam, axis=0, keepdims=True))
        return e[0:1] / jnp.sum(e, axis=0, keepdims=True)

    lb_f = lower_bound(gf_ref)
    lb_b = lower_bound(gb_ref)

    def step(t, carry):
        st_f, st_b = carry
        sl_f = pl.ds(pl.multiple_of(t * c, c), c)
        sl_b = pl.ds(pl.multiple_of((n - 1 - t) * c, c), c)
        y, st_f = _hgrn_chunk(jax.nn.silu(q_ref[0, sl_f, :]), i_ref[0, sl_f, :].astype(_BF16),
                              ff_ref[0, sl_f, :], lb_f, st_f, False)
        yf[sl_f, :] = y
        y, st_b = _hgrn_chunk(jax.nn.silu(q_ref[0, sl_b, :]), i_ref[0, sl_b, :].astype(_BF16),
                              fb_ref[0, sl_b, :], lb_b, st_b, True)
        yb[sl_b, :] = y
        return st_f, st_b

    zero = jnp.zeros((HEAD_DIM, HEAD_DIM), _F32)
    lax.fori_loop(0, n, step, (zero, zero), unroll=4)

    rows = 256

    def finish(t, carry):
        sl = pl.ds(pl.multiple_of(t * rows, rows), rows)
        y = yf[sl, :] + yb[sl, :]
        y = y * lax.rsqrt(jnp.mean(jnp.square(y), axis=-1, keepdims=True) + RMS_EPS)
        o_ref[0, sl, :] = (y * ng_ref[...] * jax.nn.silu(g_ref[0, sl, :])).astype(o_ref.dtype)
        return carry

    lax.fori_loop(0, seq // rows, finish, 0)


def _hgrn_call(proj3, gamma_f, gamma_b, norm_g, col0, b_heads):
    b, s, _ = proj3.shape
    col = lambda off: pl.BlockSpec((1, s, HEAD_DIM), lambda bi, h, off=off: (bi, 0, col0 + off + h))
    gam_spec = pl.BlockSpec((gamma_f.shape[0], HEAD_DIM), lambda bi, h: (0, h))
    return pl.pallas_call(
        functools.partial(_hgrn_kernel, seq=s),
        grid=(b, b_heads),
        in_specs=[col(0), col(b_heads), col(2 * b_heads), col(3 * b_heads), col(4 * b_heads),
                  gam_spec, gam_spec, pl.BlockSpec((1, HEAD_DIM), lambda bi, h: (0, h))],
        out_specs=pl.BlockSpec((1, s, HEAD_DIM), lambda bi, h: (bi, 0, h)),
        out_shape=jax.ShapeDtypeStruct((b, s, b_heads * HEAD_DIM), _BF16),
        scratch_shapes=[pltpu.VMEM((s, HEAD_DIM), _F32)] * 2,
        compiler_params=_params(("parallel", "parallel")),
    )(proj3, proj3, proj3, proj3, proj3, gamma_f, gamma_b, norm_g.reshape(1, -1))


def _memattn_kernel(q_ref, kv_ref, o_ref, *, m_width):
    hd = m_width // M_HEADS
    scale = hd ** -0.5
    for h in range(M_HEADS):
        q = (q_ref[0, :, h * hd:(h + 1) * hd] * scale).astype(_BF16)
        k = kv_ref[0, :, h * hd:(h + 1) * hd].astype(_BF16)
        v = kv_ref[0, :, m_width + h * hd:m_width + (h + 1) * hd].astype(_BF16)
        s = lax.dot_general(q, k, (((1,), (1,)), ((), ())), preferred_element_type=_F32)
        e = jnp.exp(s - jnp.max(s, axis=-1, keepdims=True))
        o = jnp.dot(e.astype(_BF16), v, preferred_element_type=_F32) / jnp.sum(e, axis=-1, keepdims=True)
        o_ref[0, :, h * hd:(h + 1) * hd] = o.astype(o_ref.dtype)


def _memattn_call(proj3, kvm, q_col_block, m_width):
    b, s, _ = proj3.shape
    mlen = kvm.shape[1]
    tq = _pick(s, (512, 256, 128))
    return pl.pallas_call(
        functools.partial(_memattn_kernel, m_width=m_width),
        grid=(b, s // tq),
        in_specs=[pl.BlockSpec((1, tq, m_width), lambda bi, i: (bi, i, q_col_block)),
                  pl.BlockSpec((1, mlen, 2 * m_width), lambda bi, i: (bi, 0, 0))],
        out_specs=pl.BlockSpec((1, tq, m_width), lambda bi, i: (bi, i, 0)),
        out_shape=jax.ShapeDtypeStruct((b, s, m_width), _BF16),
        compiler_params=_params(("parallel", "parallel")),
    )(proj3, kvm)


def _mix_out_kernel(a_ref, b_ref, m_ref, wa_ref, wb_ref, wm_ref, r_ref, o_ref, *, alpha):
    acc = jnp.dot(a_ref[...], wa_ref[...].astype(_BF16), preferred_element_type=_F32)
    acc += jnp.dot(b_ref[...], wb_ref[...].astype(_BF16), preferred_element_type=_F32)
    acc += jnp.dot(m_ref[...], wm_ref[...].astype(_BF16), preferred_element_type=_F32)
    o_ref[...] = alpha * r_ref[...] + acc


def _mix_out_call(oa, ob, om, w_o, resid, alpha):
    m, aw = oa.shape
    bw, mw = ob.shape[1], om.shape[1]
    n = w_o.shape[1]
    tm = _pick(m, (1024, 512, 256))
    tn = _pick(n, (512, 256, 128))
    assert aw == bw and (aw + bw) % mw == 0
    return pl.pallas_call(
        functools.partial(_mix_out_kernel, alpha=alpha),
        grid=(m // tm, n // tn),
        in_specs=[pl.BlockSpec((tm, aw), lambda i, j: (i, 0)),
                  pl.BlockSpec((tm, bw), lambda i, j: (i, 0)),
                  pl.BlockSpec((tm, mw), lambda i, j: (i, 0)),
                  pl.BlockSpec((aw, tn), lambda i, j: (0, j)),
                  pl.BlockSpec((bw, tn), lambda i, j: (1, j)),
                  pl.BlockSpec((mw, tn), lambda i, j: ((aw + bw) // mw, j)),
                  pl.BlockSpec((tm, tn), lambda i, j: (i, j))],
        out_specs=pl.BlockSpec((tm, tn), lambda i, j: (i, j)),
        out_shape=jax.ShapeDtypeStruct((m, n), _F32),
        compiler_params=_params(("parallel", "parallel")),
    )(oa, ob, om, w_o, w_o, w_o, resid)


def _ln_kernel(y_ref, g_ref, b_ref, *o_refs):
    y = y_ref[...]
    mu = jnp.mean(y, axis=-1, keepdims=True)
    d = y - mu
    var = jnp.mean(d * d, axis=-1, keepdims=True)
    out = d * lax.rsqrt(var + LN_EPS) * g_ref[...] + b_ref[...]
    for o_ref in o_refs:
        o_ref[...] = out.astype(o_ref.dtype)


def _layernorm_call(y, g, b, out_dtypes):
    m, n = y.shape
    tr = _pick(m, (256, 128, 64))
    row = pl.BlockSpec((tr, n), lambda i: (i, 0))
    vec = pl.BlockSpec((1, n), lambda i: (0, 0))
    return pl.pallas_call(
        _ln_kernel,
        grid=(m // tr,),
        in_specs=[row, vec, vec],
        out_specs=[row] * len(out_dtypes),
        out_shape=[jax.ShapeDtypeStruct((m, n), dt) for dt in out_dtypes],
        compiler_params=_params(("parallel",)),
    )(y, g.reshape(1, n), b.reshape(1, n))


def _ffn_up_kernel(xm_ref, xp_ref, xn_ref, wg_ref, wu_ref, cw_ref, cb_ref, h_ref, lhs_ref, *, tm, tiles_per_seq):
    i = pl.program_id(0)
    j = pl.program_id(1)

    @pl.when(j == 0)
    def _():
        lhs_ref[0:HALO, :] = xp_ref[...]
        lhs_ref[HALO:HALO + tm, :] = xm_ref[...]
        lhs_ref[HALO + tm:, :] = xn_ref[...]

    ext = tm + 2 * HALO
    g = jnp.dot(lhs_ref[...], wg_ref[...].astype(_BF16), preferred_element_type=_F32)
    u = jnp.dot(lhs_ref[HALO:HALO + tm, :], wu_ref[...].astype(_BF16), preferred_element_type=_F32)
    row = lax.broadcasted_iota(jnp.int32, (tm, 1), 0)
    pos = i % tiles_per_seq
    g_prev = jnp.where(jnp.logical_and(row == 0, pos == 0), 0.0,
                       pltpu.roll(g, 1, axis=0)[HALO:HALO + tm])
    g_next = jnp.where(jnp.logical_and(row == tm - 1, pos == tiles_per_seq - 1), 0.0,
                       pltpu.roll(g, ext - 1, axis=0)[HALO:HALO + tm])
    gc = cb_ref[...] + g_prev * cw_ref[0:1, :] + g[HALO:HALO + tm] * cw_ref[1:2, :] + g_next * cw_ref[2:3, :]
    h_ref[...] = (jax.nn.silu(gc) * u).astype(h_ref.dtype)


def _ffn_up_call(x1b, w_up, conv_w, conv_b, seq):
    m, d = x1b.shape
    d_ff = w_up.shape[1] // 2
    tm = _pick(seq, (1024, 512, 256))
    tn = _pick(d_ff, (256, 128))
    nj = d_ff // tn
    tiles_per_seq = seq // tm
    hb = tm // HALO
    last_hb = m // HALO - 1
    return pl.pallas_call(
        functools.partial(_ffn_up_kernel, tm=tm, tiles_per_seq=tiles_per_seq),
        grid=(m // tm, nj),
        in_specs=[pl.BlockSpec((tm, d), lambda i, j: (i, 0)),
                  pl.BlockSpec((HALO, d), lambda i, j: (jnp.maximum(i * hb - 1, 0), 0)),
                  pl.BlockSpec((HALO, d), lambda i, j: (jnp.minimum((i + 1) * hb, last_hb), 0)),
                  pl.BlockSpec((d, tn), lambda i, j: (0, j)),
                  pl.BlockSpec((d, tn), lambda i, j: (0, j + nj)),
                  pl.BlockSpec((CONV_WIDTH, tn), lambda i, j: (0, j)),
                  pl.BlockSpec((1, tn), lambda i, j: (0, j))],
        out_specs=pl.BlockSpec((tm, tn), lambda i, j: (i, j)),
        out_shape=jax.ShapeDtypeStruct((m, d_ff), _BF16),
        scratch_shapes=[pltpu.VMEM((tm + 2 * HALO, d), _BF16)],
        compiler_params=_params(("parallel", "arbitrary")),
    )(x1b, x1b, x1b, w_up, w_up, conv_w, conv_b.reshape(1, d_ff))


def _rope_tables(positions):
    half = HEAD_DIM // 2
    inv = 1.0 / (ROPE_THETA ** (jnp.arange(0, HEAD_DIM, 2, dtype=_F32) / HEAD_DIM))
    ang = positions.astype(_F32)[..., None] * inv
    cos, sin = jnp.cos(ang), jnp.sin(ang)
    return jnp.concatenate([cos, cos], axis=-1), jnp.concatenate([-sin, sin], axis=-1)


def kernel(x, mem, positions, w_in, w_mem_kv, w_o, hgrn_gamma_fwd, hgrn_gamma_bwd, hgrn_norm_g,
           ln1_g, ln1_b, w_up, conv_w, conv_b, w_down, ln2_g, ln2_b):
    b, s, d = x.shape
    depth = w_in.shape[0]
    a_width = b_width = 3 * d // 8
    m_width = d // 4
    a_heads = a_width // HEAD_DIM
    b_heads = b_width // HEAD_DIM
    alpha = (2 * depth) ** 0.25
    m = b * s
    cosf, sinf = _rope_tables(positions)
    tm = _pick(m, (1024, 512, 256))

    xf = x.reshape(m, d)
    for layer in range(depth):
        assert layer == 0, "lower-bound running sum is implemented for the first layer only"
        xb = xf.astype(_BF16)
        proj = _matmul(xb, w_in[layer], _F32, tm, _pick(w_in.shape[2], (512, 256, 128)))
        proj3 = proj.reshape(b, s, -1)
        o_a = _attn_call(proj3, cosf, sinf, a_heads)
        o_b = _hgrn_call(proj3, hgrn_gamma_fwd, hgrn_gamma_bwd, hgrn_norm_g[layer],
                         3 * a_heads, b_heads)
        kvm = _matmul(mem.reshape(-1, d), w_mem_kv[layer], _F32,
                      _pick(b * mem.shape[1], (256, 128)), _pick(2 * m_width, (512, 256, 128)))
        o_m = _memattn_call(proj3, kvm.reshape(b, mem.shape[1], 2 * m_width),
                            (3 * a_width + 5 * b_width) // m_width, m_width)
        y1 = _mix_out_call(o_a.reshape(m, a_width), o_b.reshape(m, b_width), o_m.reshape(m, m_width),
                           w_o[layer], xf, alpha)
        x1, x1b = _layernorm_call(y1, ln1_g[layer], ln1_b[layer], (_F32, _BF16))
        h = _ffn_up_call(x1b, w_up[layer], conv_w[layer], conv_b[layer], s)
        y2 = _matmul_residual(h, w_down[layer], x1, alpha,
                              _pick(m, (2048, 1024, 512, 256)), _pick(d, (1024, 512, 256)), 512)
        (xf,) = _layernorm_call(y2, ln2_g[layer], ln2_b[layer], (_F32,))
    return xf.reshape(b, s, d)
```

```python
import functools

import jax
import jax.numpy as jnp
from jax import lax
from jax.experimental import pallas as pl
from jax.experimental.pallas import tpu as pltpu

HEAD_DIM = 128
DILATED_PATTERNS = ((128, 1), (512, 4), (2048, 16))
HGRN_CHUNK = 64
HGRN_SUB = 16
HGRN_EXP_CLAMP = 60.0
M_HEADS = 4
CONV_WIDTH = 3
ROPE_THETA = 10000.0
LN_EPS = 1e-5
RMS_EPS = 1e-6
NEG_INF = -1e30
LOG2_E = 1.4426950408889634
HALO = 16
VMEM_LIMIT = 56 * 1024 * 1024

_BF16 = jnp.bfloat16
_F32 = jnp.float32


def _pick(n, prefs):
    for p in prefs:
        if n % p == 0:
            return p
    return n


def _params(sem):
    return pltpu.CompilerParams(dimension_semantics=sem, vmem_limit_bytes=VMEM_LIMIT)


def _mm_kernel(x_ref, w_ref, o_ref):
    o_ref[...] = jnp.dot(x_ref[...].astype(_BF16), w_ref[...].astype(_BF16),
                         preferred_element_type=_F32).astype(o_ref.dtype)


def _mm_acc_kernel(x_ref, w_ref, o_ref, acc_ref, *, nk, tail):
    k = pl.program_id(2)
    tk = x_ref.shape[1]

    def part(valid):
        x, w = x_ref[...], w_ref[...]
        if valid < tk:
            col = lax.broadcasted_iota(jnp.int32, x.shape, 1)
            x = jnp.where(col < valid, x.astype(_F32), 0.0)
            row = lax.broadcasted_iota(jnp.int32, w.shape, 0)
            w = jnp.where(row < valid, w, 0.0)
        return jnp.dot(x.astype(_BF16), w.astype(_BF16), preferred_element_type=_F32)

    @pl.when(k == 0)
    def _():
        acc_ref[...] = part(tk)

    @pl.when(jnp.logical_and(k > 0, k < nk - 1))
    def _():
        acc_ref[...] += part(tk)

    @pl.when(k == nk - 1)
    def _():
        o_ref[...] = acc_ref[...] + part(tail)


def _matmul(x, w, out_dtype, tm, tn):
    m, kdim = x.shape
    n = w.shape[1]
    return pl.pallas_call(
        _mm_kernel,
        grid=(m // tm, n // tn),
        in_specs=[pl.BlockSpec((tm, kdim), lambda i, j: (i, 0)),
                  pl.BlockSpec((kdim, tn), lambda i, j: (0, j))],
        out_specs=pl.BlockSpec((tm, tn), lambda i, j: (i, j)),
        out_shape=jax.ShapeDtypeStruct((m, n), out_dtype),
        compiler_params=_params(("parallel", "parallel")),
    )(x, w)


def _matmul_ktiled(x, w, tm, tn, tk):
    m, kdim = x.shape
    n = w.shape[1]
    nk = pl.cdiv(kdim, tk)
    assert nk >= 2
    return pl.pallas_call(
        functools.partial(_mm_acc_kernel, nk=nk, tail=kdim - (nk - 1) * tk),
        grid=(m // tm, n // tn, nk),
        in_specs=[pl.BlockSpec((tm, tk), lambda i, j, k: (i, k)),
                  pl.BlockSpec((tk, tn), lambda i, j, k: (k, j))],
        out_specs=pl.BlockSpec((tm, tn), lambda i, j, k: (i, j)),
        out_shape=jax.ShapeDtypeStruct((m, n), _F32),
        scratch_shapes=[pltpu.VMEM((tm, tn), _F32)],
        compiler_params=_params(("parallel", "parallel", "arbitrary")),
    )(x, w)


def _attn_kernel(q_ref, k_ref, v_ref, cos_ref, sin_ref, o_ref, qn, kn, qm, km, vm, qs, ks, vs,
                 run0, run1, bias, *, seq, bq, group):
    scale = HEAD_DIM ** -0.5 * LOG2_E
    (w0, d0), (w1, d1), (w2, d2) = sorted(DILATED_PATTERNS, key=lambda wd: wd[1])
    assert d0 == 1 and d1 % d0 == 0 and d2 % d1 == 0
    r1, r2 = d1 // d0, d2 // d1
    len1, len2 = seq // d1, seq // d2
    rows = 256
    rows2 = min(rows, len2)

    def rope_rows(t, carry):
        sl = pl.ds(pl.multiple_of(t * rows, rows), rows)
        cos, sin = cos_ref[0, sl, :], sin_ref[0, sl, :]
        qv, kv = q_ref[0, sl, :], k_ref[0, sl, :]
        qn[sl, :] = (qv * cos + pltpu.roll(qv, HEAD_DIM // 2, axis=1) * sin) * scale
        kn[sl, :] = kv * cos + pltpu.roll(kv, HEAD_DIM // 2, axis=1) * sin
        return carry

    lax.fori_loop(0, seq // rows, rope_rows, 0)

    def build_mid(t, carry):
        per = len1 // rows
        r, c = t // per, t % per
        src = pl.ds(r + r1 * c * rows, rows, stride=r1)
        dst = pl.ds(pl.multiple_of(r * len1 + c * rows, rows), rows)
        qm[dst, :] = qn[src, :]
        km[dst, :] = kn[src, :]
        vm[dst, :] = v_ref[0, src, :]
        return carry

    lax.fori_loop(0, d1 * (len1 // rows), build_mid, 0)

    def build_top(t, carry):
        per = len2 // rows2
        r, c = t // per, t % per
        src = pl.ds((r % d1) * len1 + r // d1 + r2 * c * rows2, rows2, stride=r2)
        dst = pl.ds(pl.multiple_of(r * len2 + c * rows2, rows2), rows2)
        qs[dst, :] = qm[src, :].astype(_BF16)
        ks[dst, :] = km[src, :].astype(_BF16)
        vs[dst, 0:HEAD_DIM] = vm[src, :].astype(_BF16)
        vs[dst, HEAD_DIM:] = jnp.ones((rows2, HEAD_DIM), _BF16)
        return carry

    lax.fori_loop(0, d2 * (len2 // rows2), build_top, 0)

    def cast_rows(q_src, k_src, v_src):
        def body(t, carry):
            sl = pl.ds(pl.multiple_of(t * rows, rows), rows)
            qs[sl, :] = q_src(sl).astype(_BF16)
            ks[sl, :] = k_src(sl).astype(_BF16)
            vs[sl, 0:HEAD_DIM] = v_src(sl).astype(_BF16)
            return carry
        lax.fori_loop(0, seq // rows, body, 0)

    def branch(window, dil, parent, prev, store):
        sub_len = seq // dil
        half = window // (2 * dil)
        win = min(bq + 2 * half, sub_len)
        per = sub_len // bq
        offsets = {q0 - min(max(q0 - half, 0), sub_len - win) for q0 in range(0, sub_len, bq)}
        assert offsets <= {0, half, 2 * half} and half % 16 == 0 and sub_len % bq == 0

        for v in range(3):
            rel = (lax.broadcasted_iota(jnp.int32, (bq, win), 0)
                   - lax.broadcasted_iota(jnp.int32, (bq, win), 1) + v * half)
            bias[v, :, 0:win] = jnp.where(jnp.abs(rel) <= half, 0.0, NEG_INF)

        def block(t):
            r = t // per
            q0 = (t % per) * bq
            ws = jnp.clip(q0 - half, 0, sub_len - win)
            base = r * sub_len
            own = pl.ds(pl.multiple_of(base + q0, bq), bq)
            keys = pl.ds(pl.multiple_of(base + ws, half), win)
            s = lax.dot_general(qs[own, :], ks[keys, :], (((1,), (1,)), ((), ())),
                                preferred_element_type=_F32) + bias[(q0 - ws) // half, :, 0:win]
            m = jnp.broadcast_to(jnp.max(s, axis=-1, keepdims=True), (bq, HEAD_DIM))
            if prev is not None:
                m_p = prev[1, own, :]
                m = jnp.maximum(m, m_p)
            e = jnp.concatenate([jnp.exp2(s[:, c:c + HEAD_DIM] - m) for c in range(0, win, HEAD_DIM)],
                                axis=1)
            pv = jnp.dot(e.astype(_BF16), vs[keys, :], preferred_element_type=_F32)
            acc, l = pv[:, 0:HEAD_DIM], pv[:, HEAD_DIM:]
            if prev is not None:
                a = jnp.exp2(m_p - m)
                acc = a * prev[0, own, :] + acc
                l = a * prev[2, own, :] + l
            if store is None:
                o_ref[0, own, :] = (acc / l).astype(o_ref.dtype)
            else:
                pd, plen = parent
                ratio = dil // pd
                dst = pl.ds((r % pd) * plen + r // pd + ratio * q0, bq, stride=ratio)
                store[0, dst, :] = acc
                store[1, dst, :] = m
                store[2, dst, :] = l

        def body(tt, carry):
            for g in range(group):
                block(tt * group + g)
            return carry

        lax.fori_loop(0, seq // (bq * group), body, 0)

    branch(w2, d2, (d1, len1), None, run1)
    cast_rows(lambda sl: qm[sl, :], lambda sl: km[sl, :], lambda sl: vm[sl, :])
    branch(w1, d1, (d0, seq), run1, run0)
    cast_rows(lambda sl: qn[sl, :], lambda sl: kn[sl, :], lambda sl: v_ref[0, sl, :])
    branch(w0, d0, None, run0, None)


def _attn_call(proj3, cosf, sinf, a_heads):
    b, s, _ = proj3.shape
    bq = 128
    half_max = max(w // (2 * d) for w, d in DILATED_PATTERNS)
    col = lambda off: pl.BlockSpec((1, s, HEAD_DIM), lambda bi, h, off=off: (bi, 0, off + h))
    rope_spec = pl.BlockSpec((1, s, HEAD_DIM), lambda bi, h: (bi, 0, 0), pipeline_mode=pl.Buffered(1))
    seq_f32 = pltpu.VMEM((s, HEAD_DIM), _F32)
    seq_bf16 = pltpu.VMEM((s, HEAD_DIM), _BF16)
    return pl.pallas_call(
        functools.partial(_attn_kernel, seq=s, bq=bq, group=16),
        grid=(b, a_heads),
        in_specs=[col(0), col(a_heads), col(2 * a_heads), rope_spec, rope_spec],
        out_specs=pl.BlockSpec((1, s, HEAD_DIM), lambda bi, h: (bi, 0, h)),
        out_shape=jax.ShapeDtypeStruct((b, s, a_heads * HEAD_DIM), _BF16),
        scratch_shapes=[seq_f32] * 5 + [seq_bf16] * 2 + [pltpu.VMEM((s, 2 * HEAD_DIM), _BF16)]
        + [pltpu.VMEM((3, s, HEAD_DIM), _F32)] * 2
        + [pltpu.VMEM((3, bq, bq + 2 * half_max), _F32)],
        compiler_params=_params(("parallel", "parallel")),
    )(proj3, proj3, proj3, cosf, sinf)


def _scan_rows(x, reverse):
    n = x.shape[0]
    row = lax.broadcasted_iota(jnp.int32, x.shape, 0)
    k = 1
    while k < n:
        if reverse:
            x = x + jnp.where(row < n - k, pltpu.roll(x, n - k, axis=0), 0.0)
        else:
            x = x + jnp.where(row >= k, pltpu.roll(x, k, axis=0), 0.0)
        k *= 2
    return x


def _hgrn_chunk(qc, vc16, fp, lb, state_t, reverse):
    c = HGRN_CHUNK
    f = lb + (1.0 - lb) * jax.nn.sigmoid(fp)
    kk = 1.0 - f
    bc = _scan_rows(jnp.log(f), reverse)
    inter = lax.dot_general((qc * jnp.exp(bc)).astype(_BF16), state_t.astype(_BF16),
                            (((1,), (1,)), ((), ())), preferred_element_type=_F32)
    blocks = []
    for i in range(c // HGRN_SUB):
        lo, hi = i * HGRN_SUB, (i + 1) * HGRN_SUB
        beta = bc[hi - 1:hi] if reverse else bc[lo:lo + 1]
        qt = qc[lo:hi] * jnp.exp(bc[lo:hi] - beta)
        kt = kk * jnp.exp(jnp.minimum(beta - bc, HGRN_EXP_CLAMP))
        blocks.append(lax.dot_general(qt.astype(_BF16), kt.astype(_BF16),
                                      (((1,), (1,)), ((), ())), preferred_element_type=_F32))
    scores = jnp.concatenate(blocks, axis=0)
    t_idx = lax.broadcasted_iota(jnp.int32, (c, c), 0)
    s_idx = lax.broadcasted_iota(jnp.int32, (c, c), 1)
    keep = (s_idx >= t_idx) if reverse else (s_idx <= t_idx)
    scores = jnp.where(keep, scores, 0.0)
    intra = jnp.dot(scores.astype(_BF16), vc16, preferred_element_type=_F32)
    last = bc[0:1] if reverse else bc[c - 1:c]
    kd = (kk * jnp.exp(last - bc)).astype(_BF16)
    new_state = jnp.exp(last) * state_t + lax.dot_general(
        vc16, kd, (((0,), (0,)), ((), ())), preferred_element_type=_F32)
    return inter + intra, new_state


def _hgrn_kernel(q_ref, i_ref, g_ref, ff_ref, fb_ref, gf_ref, gb_ref, ng_ref, o_ref, yf, yb, *, seq):
    c = HGRN_CHUNK
    n = seq // c

    def lower_bound(gam_ref):
        gam = gam_ref[...]
        e = jnp.exp(gam - jnp.max(gam, axis=0, keepdims=True))
        return e[0:1] / jnp.sum(e, axis=0, keepdims=True)

    lb_f = lower_bound(gf_ref)
    lb_b = lower_bound(gb_ref)

    def step(t, carry):
        st_f, st_b = carry
        sl_f = pl.ds(pl.multiple_of(t * c, c), c)
        sl_b = pl.ds(pl.multiple_of((n - 1 - t) * c, c), c)
        y, st_f = _hgrn_chunk(jax.nn.silu(q_ref[0, sl_f, :]), i_ref[0, sl_f, :].astype(_BF16),
                              ff_ref[0, sl_f, :], lb_f, st_f, False)
        yf[sl_f, :] = y
        y, st_b = _hgrn_chunk(jax.nn.silu(q_ref[0, sl_b, :]), i_ref[0, sl_b, :].astype(_BF16),
                              fb_ref[0, sl_b, :], lb_b, st_b, True)
        yb[sl_b, :] = y
        return st_f, st_b

    zero = jnp.zeros((HEAD_DIM, HEAD_DIM), _F32)
    lax.fori_loop(0, n, step, (zero, zero), unroll=4)

    rows = 256

    def finish(t, carry):
        sl = pl.ds(pl.multiple_of(t * rows, rows), rows)
        y = yf[sl, :] + yb[sl, :]
        y = y * lax.rsqrt(jnp.mean(jnp.square(y), axis=-1, keepdims=True) + RMS_EPS)
        o_ref[0, sl, :] = (y * ng_ref[...] * jax.nn.silu(g_ref[0, sl, :])).astype(o_ref.dtype)
        return carry

    lax.fori_loop(0, seq // rows, finish, 0)


def _hgrn_call(proj3, gamma_f, gamma_b, norm_g, col0, b_heads):
    b, s, _ = proj3.shape
    col = lambda off: pl.BlockSpec((1, s, HEAD_DIM), lambda bi, h, off=off: (bi, 0, col0 + off + h))
    gam_spec = pl.BlockSpec((gamma_f.shape[0], HEAD_DIM), lambda bi, h: (0, h))
    return pl.pallas_call(
        functools.partial(_hgrn_kernel, seq=s),
        grid=(b, b_heads),
        in_specs=[col(0), col(b_heads), col(2 * b_heads), col(3 * b_heads), col(4 * b_heads),
                  gam_spec, gam_spec, pl.BlockSpec((1, HEAD_DIM), lambda bi, h: (0, h))],
        out_specs=pl.BlockSpec((1, s, HEAD_DIM), lambda bi, h: (bi, 0, h)),
        out_shape=jax.ShapeDtypeStruct((b, s, b_heads * HEAD_DIM), _BF16),
        scratch_shapes=[pltpu.VMEM((s, HEAD_DIM), _F32)] * 2,
        compiler_params=_params(("parallel", "parallel")),
    )(proj3, proj3, proj3, proj3, proj3, gamma_f, gamma_b, norm_g.reshape(1, -1))


def _memattn_kernel(q_ref, kv_ref, o_ref, *, m_width):
    hd = m_width // M_HEADS
    scale = hd ** -0.5
    for h in range(M_HEADS):
        q = (q_ref[0, :, h * hd:(h + 1) * hd] * scale).astype(_BF16)
        k = kv_ref[0, :, h * hd:(h + 1) * hd].astype(_BF16)
        v = kv_ref[0, :, m_width + h * hd:m_width + (h + 1) * hd].astype(_BF16)
        s = lax.dot_general(q, k, (((1,), (1,)), ((), ())), preferred_element_type=_F32)
        e = jnp.exp(s - jnp.max(s, axis=-1, keepdims=True))
        o = jnp.dot(e.astype(_BF16), v, preferred_element_type=_F32) / jnp.sum(e, axis=-1, keepdims=True)
        o_ref[0, :, h * hd:(h + 1) * hd] = o.astype(o_ref.dtype)


def _memattn_call(proj3, kvm, q_col_block, m_width):
    b, s, _ = proj3.shape
    mlen = kvm.shape[1]
    tq = _pick(s, (512, 256, 128))
    return pl.pallas_call(
        functools.partial(_memattn_kernel, m_width=m_width),
        grid=(b, s // tq),
        in_specs=[pl.BlockSpec((1, tq, m_width), lambda bi, i: (bi, i, q_col_block)),
                  pl.BlockSpec((1, mlen, 2 * m_width), lambda bi, i: (bi, 0, 0))],
        out_specs=pl.BlockSpec((1, tq, m_width), lambda bi, i: (bi, i, 0)),
        out_shape=jax.ShapeDtypeStruct((b, s, m_width), _BF16),
        compiler_params=_params(("parallel", "parallel")),
    )(proj3, kvm)


def _mix_out_kernel(a_ref, b_ref, m_ref, wa_ref, wb_ref, wm_ref, r_ref, o_ref, *, alpha):
    acc = jnp.dot(a_ref[...], wa_ref[...].astype(_BF16), preferred_element_type=_F32)
    acc += jnp.dot(b_ref[...], wb_ref[...].astype(_BF16), preferred_element_type=_F32)
    acc += jnp.dot(m_ref[...], wm_ref[...].astype(_BF16), preferred_element_type=_F32)
    o_ref[...] = alpha * r_ref[...] + acc


def _mix_out_call(oa, ob, om, w_o, resid, alpha):
    m, aw = oa.shape
    bw, mw = ob.shape[1], om.shape[1]
    n = w_o.shape[1]
    tm = _pick(m, (1024, 512, 256))
    tn = _pick(n, (512, 256, 128))
    assert aw == bw and (aw + bw) % mw == 0
    return pl.pallas_call(
        functools.partial(_mix_out_kernel, alpha=alpha),
        grid=(m // tm, n // tn),
        in_specs=[pl.BlockSpec((tm, aw), lambda i, j: (i, 0)),
                  pl.BlockSpec((tm, bw), lambda i, j: (i, 0)),
                  pl.BlockSpec((tm, mw), lambda i, j: (i, 0)),
                  pl.BlockSpec((aw, tn), lambda i, j: (0, j)),
                  pl.BlockSpec((bw, tn), lambda i, j: (1, j)),
                  pl.BlockSpec((mw, tn), lambda i, j: ((aw + bw) // mw, j)),
                  pl.BlockSpec((tm, tn), lambda i, j: (i, j))],
        out_specs=pl.BlockSpec((tm, tn), lambda i, j: (i, j)),
        out_shape=jax.ShapeDtypeStruct((m, n), _F32),
        compiler_params=_params(("parallel", "parallel")),
    )(oa, ob, om, w_o, w_o, w_o, resid)


def _ln_kernel(*refs, alpha):
    if alpha is None:
        y_ref, g_ref, b_ref, *o_refs = refs
        y = y_ref[...]
    else:
        y_ref, r_ref, g_ref, b_ref, *o_refs = refs
        y = alpha * r_ref[...] + y_ref[...]
    mu = jnp.mean(y, axis=-1, keepdims=True)
    d = y - mu
    var = jnp.mean(d * d, axis=-1, keepdims=True)
    out = d * lax.rsqrt(var + LN_EPS) * g_ref[...] + b_ref[...]
    for o_ref in o_refs:
        o_ref[...] = out.astype(o_ref.dtype)


def _layernorm_call(y, g, b, out_dtypes, resid=None, alpha=None):
    m, n = y.shape
    tr = _pick(m, (256, 128, 64))
    row = pl.BlockSpec((tr, n), lambda i: (i, 0))
    vec = pl.BlockSpec((1, n), lambda i: (0, 0))
    rows_in = (y,) if resid is None else (y, resid)
    return pl.pallas_call(
        functools.partial(_ln_kernel, alpha=None if resid is None else alpha),
        grid=(m // tr,),
        in_specs=[row] * len(rows_in) + [vec, vec],
        out_specs=[row] * len(out_dtypes),
        out_shape=[jax.ShapeDtypeStruct((m, n), dt) for dt in out_dtypes],
        compiler_params=_params(("parallel",)),
    )(*rows_in, g.reshape(1, n), b.reshape(1, n))


def _ffn_up_kernel(x_hbm, wg_ref, wu_ref, cw_ref, cb_ref, h_ref, lhs_ref, sem, *, tm, tiles_per_seq):
    i = pl.program_id(0)
    j = pl.program_id(1)

    @pl.when(j == 0)
    def _():
        pos = i % tiles_per_seq
        row0 = pl.multiple_of(i * tm, tm)
        main = pltpu.make_async_copy(x_hbm.at[pl.ds(row0, tm), :], lhs_ref.at[pl.ds(HALO, tm), :],
                                     sem.at[0])
        main.start()

        @pl.when(pos > 0)
        def _():
            cp = pltpu.make_async_copy(x_hbm.at[pl.ds(row0 - HALO, HALO), :],
                                       lhs_ref.at[pl.ds(0, HALO), :], sem.at[1])
            cp.start()
            cp.wait()

        @pl.when(pos == 0)
        def _():
            lhs_ref[0:HALO, :] = jnp.zeros((HALO, lhs_ref.shape[1]), lhs_ref.dtype)

        @pl.when(pos < tiles_per_seq - 1)
        def _():
            cp = pltpu.make_async_copy(x_hbm.at[pl.ds(row0 + tm, HALO), :],
                                       lhs_ref.at[pl.ds(HALO + tm, HALO), :], sem.at[2])
            cp.start()
            cp.wait()

        @pl.when(pos == tiles_per_seq - 1)
        def _():
            lhs_ref[HALO + tm:, :] = jnp.zeros((HALO, lhs_ref.shape[1]), lhs_ref.dtype)

        main.wait()

    ext = tm + 2 * HALO
    g = jnp.dot(lhs_ref[...], wg_ref[...].astype(_BF16), preferred_element_type=_F32)
    u = jnp.dot(lhs_ref[HALO:HALO + tm, :], wu_ref[...].astype(_BF16), preferred_element_type=_F32)
    g_prev = pltpu.roll(g, 1, axis=0)[HALO:HALO + tm]
    g_next = pltpu.roll(g, ext - 1, axis=0)[HALO:HALO + tm]
    gc = cb_ref[...] + g_prev * cw_ref[0:1, :] + g[HALO:HALO + tm] * cw_ref[1:2, :] + g_next * cw_ref[2:3, :]
    h_ref[...] = (jax.nn.silu(gc) * u).astype(h_ref.dtype)


def _ffn_up_call(x1b, w_up, conv_w, conv_b, seq):
    m, d = x1b.shape
    d_ff = w_up.shape[1] // 2
    tm = _pick(seq, (2048, 1024, 512, 256))
    tn = _pick(d_ff, (256, 128))
    nj = d_ff // tn
    return pl.pallas_call(
        functools.partial(_ffn_up_kernel, tm=tm, tiles_per_seq=seq // tm),
        grid=(m // tm, nj),
        in_specs=[pl.BlockSpec(memory_space=pl.ANY),
                  pl.BlockSpec((d, tn), lambda i, j: (0, j)),
                  pl.BlockSpec((d, tn), lambda i, j: (0, j + nj)),
                  pl.BlockSpec((CONV_WIDTH, tn), lambda i, j: (0, j)),
                  pl.BlockSpec((1, tn), lambda i, j: (0, j))],
        out_specs=pl.BlockSpec((tm, tn), lambda i, j: (i, j)),
        out_shape=jax.ShapeDtypeStruct((m, d_ff), _BF16),
        scratch_shapes=[pltpu.VMEM((tm + 2 * HALO, d), _BF16), pltpu.SemaphoreType.DMA((3,))],
        compiler_params=_params(("arbitrary", "arbitrary")),
    )(x1b, w_up, w_up, conv_w, conv_b.reshape(1, d_ff))


def _rope_tables(positions):
    half = HEAD_DIM // 2
    inv = 1.0 / (ROPE_THETA ** (jnp.arange(0, HEAD_DIM, 2, dtype=_F32) / HEAD_DIM))
    ang = positions.astype(_F32)[..., None] * inv
    cos, sin = jnp.cos(ang), jnp.sin(ang)
    return jnp.concatenate([cos, cos], axis=-1), jnp.concatenate([-sin, sin], axis=-1)


def kernel(x, mem, positions, w_in, w_mem_kv, w_o, hgrn_gamma_fwd, hgrn_gamma_bwd, hgrn_norm_g,
           ln1_g, ln1_b, w_up, conv_w, conv_b, w_down, ln2_g, ln2_b):
    b, s, d = x.shape
    depth = w_in.shape[0]
    a_width = b_width = 3 * d // 8
    m_width = d // 4
    a_heads = a_width // HEAD_DIM
    b_heads = b_width // HEAD_DIM
    alpha = (2 * depth) ** 0.25
    m = b * s
    cosf, sinf = _rope_tables(positions)
    tm = _pick(m, (1024, 512, 256))

    xf = x.reshape(m, d)
    for layer in range(depth):
        assert layer == 0, "lower-bound running sum is implemented for the first layer only"
        xb = xf.astype(_BF16)
        proj = _matmul(xb, w_in[layer], _F32, tm, _pick(w_in.shape[2], (512, 256, 128)))
        proj3 = proj.reshape(b, s, -1)
        o_a = _attn_call(proj3, cosf, sinf, a_heads)
        o_b = _hgrn_call(proj3, hgrn_gamma_fwd, hgrn_gamma_bwd, hgrn_norm_g[layer],
                         3 * a_heads, b_heads)
        kvm = _matmul(mem.reshape(-1, d), w_mem_kv[layer], _F32,
                      _pick(b * mem.shape[1], (256, 128)), _pick(2 * m_width, (512, 256, 128)))
        o_m = _memattn_call(proj3, kvm.reshape(b, mem.shape[1], 2 * m_width),
                            (3 * a_width + 5 * b_width) // m_width, m_width)
        y1 = _mix_out_call(o_a.reshape(m, a_width), o_b.reshape(m, b_width), o_m.reshape(m, m_width),
                           w_o[layer], xf, alpha)
        x1, x1b = _layernorm_call(y1, ln1_g[layer], ln1_b[layer], (_F32, _BF16))
        h = _ffn_up_call(x1b, w_up[layer], conv_w[layer], conv_b[layer], s)
        ffn = _matmul_ktiled(h, w_down[layer], _pick(m, (2048, 1024, 512, 256)),
                             _pick(d, (1024, 512, 256)), 1024)
        (xf,) = _layernorm_call(ffn, ln2_g[layer], ln2_b[layer], (_F32,), resid=x1, alpha=alpha)
    return xf.reshape(b, s, d)
```

```python
import functools

import jax
import jax.numpy as jnp
from jax import lax
from jax.experimental import pallas as pl
from jax.experimental.pallas import tpu as pltpu

HEAD_DIM = 128
DILATED_PATTERNS = ((128, 1), (512, 4), (2048, 16))
HGRN_CHUNK = 64
HGRN_SUB = 16
HGRN_EXP_CLAMP = 60.0
M_HEADS = 4
CONV_WIDTH = 3
ROPE_THETA = 10000.0
LN_EPS = 1e-5
RMS_EPS = 1e-6
NEG_INF = -1e30
LOG2_E = 1.4426950408889634
HALO = 16
VMEM_LIMIT = 56 * 1024 * 1024

_BF16 = jnp.bfloat16
_F32 = jnp.float32


def _pick(n, prefs):
    for p in prefs:
        if n % p == 0:
            return p
    return n


def _params(sem):
    return pltpu.CompilerParams(dimension_semantics=sem, vmem_limit_bytes=VMEM_LIMIT)


def _mm_kernel(x_ref, w_ref, o_ref):
    o_ref[...] = jnp.dot(x_ref[...].astype(_BF16), w_ref[...].astype(_BF16),
                         preferred_element_type=_F32).astype(o_ref.dtype)


def _mm_acc_kernel(x_ref, w_ref, o_ref, acc_ref, *, nk, tail):
    k = pl.program_id(2)
    tk = x_ref.shape[1]

    def part(valid):
        x, w = x_ref[...], w_ref[...]
        if valid < tk:
            col = lax.broadcasted_iota(jnp.int32, x.shape, 1)
            x = jnp.where(col < valid, x.astype(_F32), 0.0)
            row = lax.broadcasted_iota(jnp.int32, w.shape, 0)
            w = jnp.where(row < valid, w, 0.0)
        return jnp.dot(x.astype(_BF16), w.astype(_BF16), preferred_element_type=_F32)

    @pl.when(k == 0)
    def _():
        acc_ref[...] = part(tk)

    @pl.when(jnp.logical_and(k > 0, k < nk - 1))
    def _():
        acc_ref[...] += part(tk)

    @pl.when(k == nk - 1)
    def _():
        o_ref[...] = acc_ref[...] + part(tail)


def _matmul(x, w, out_dtype, tm, tn):
    m, kdim = x.shape
    n = w.shape[1]
    return pl.pallas_call(
        _mm_kernel,
        grid=(m // tm, n // tn),
        in_specs=[pl.BlockSpec((tm, kdim), lambda i, j: (i, 0)),
                  pl.BlockSpec((kdim, tn), lambda i, j: (0, j))],
        out_specs=pl.BlockSpec((tm, tn), lambda i, j: (i, j)),
        out_shape=jax.ShapeDtypeStruct((m, n), out_dtype),
        compiler_params=_params(("parallel", "parallel")),
    )(x, w)


def _mm_rows_kernel(x_hbm, w_ref, o_ref, x_scr, sem, *, tm):
    i = pl.program_id(0)

    @pl.when(pl.program_id(1) == 0)
    def _():
        cp = pltpu.make_async_copy(x_hbm.at[pl.ds(pl.multiple_of(i * tm, tm), tm), :], x_scr, sem.at[0])
        cp.start()
        cp.wait()

    o_ref[...] = jnp.dot(x_scr[...], w_ref[...].astype(_BF16),
                         preferred_element_type=_F32).astype(o_ref.dtype)


def _matmul_rows_resident(x, w, out_dtype, tm, tn):
    m, kdim = x.shape
    n = w.shape[1]
    return pl.pallas_call(
        functools.partial(_mm_rows_kernel, tm=tm),
        grid=(m // tm, n // tn),
        in_specs=[pl.BlockSpec(memory_space=pl.ANY),
                  pl.BlockSpec((kdim, tn), lambda i, j: (0, j))],
        out_specs=pl.BlockSpec((tm, tn), lambda i, j: (i, j)),
        out_shape=jax.ShapeDtypeStruct((m, n), out_dtype),
        scratch_shapes=[pltpu.VMEM((tm, kdim), x.dtype), pltpu.SemaphoreType.DMA((1,))],
        compiler_params=_params(("arbitrary", "arbitrary")),
    )(x, w)


def _matmul_ktiled(x, w, tm, tn, tk):
    m, kdim = x.shape
    n = w.shape[1]
    nk = pl.cdiv(kdim, tk)
    assert nk >= 2
    return pl.pallas_call(
        functools.partial(_mm_acc_kernel, nk=nk, tail=kdim - (nk - 1) * tk),
        grid=(m // tm, n // tn, nk),
        in_specs=[pl.BlockSpec((tm, tk), lambda i, j, k: (i, k)),
                  pl.BlockSpec((tk, tn), lambda i, j, k: (k, j))],
        out_specs=pl.BlockSpec((tm, tn), lambda i, j, k: (i, j)),
        out_shape=jax.ShapeDtypeStruct((m, n), _F32),
        scratch_shapes=[pltpu.VMEM((tm, tn), _F32)],
        compiler_params=_params(("parallel", "parallel", "arbitrary")),
    )(x, w)


def _attn_kernel(q_ref, k_ref, v_ref, cos_ref, sin_ref, o_ref, qn, kn, qm, km, vm, qs, ks, vs,
                 run0, run1, bias, *, seq, bq, group):
    scale = HEAD_DIM ** -0.5 * LOG2_E
    (w0, d0), (w1, d1), (w2, d2) = sorted(DILATED_PATTERNS, key=lambda wd: wd[1])
    assert d0 == 1 and d1 % d0 == 0 and d2 % d1 == 0
    r1, r2 = d1 // d0, d2 // d1
    len1, len2 = seq // d1, seq // d2
    rows = 256
    rows2 = min(rows, len2)

    def rope_rows(t, carry):
        sl = pl.ds(pl.multiple_of(t * rows, rows), rows)
        cos, sin = cos_ref[0, sl, :], sin_ref[0, sl, :]
        qv, kv = q_ref[0, sl, :], k_ref[0, sl, :]
        qn[sl, :] = (qv * cos + pltpu.roll(qv, HEAD_DIM // 2, axis=1) * sin) * scale
        kn[sl, :] = kv * cos + pltpu.roll(kv, HEAD_DIM // 2, axis=1) * sin
        return carry

    lax.fori_loop(0, seq // rows, rope_rows, 0)

    def build_mid(t, carry):
        per = len1 // rows
        r, c = t // per, t % per
        src = pl.ds(r + r1 * c * rows, rows, stride=r1)
        dst = pl.ds(pl.multiple_of(r * len1 + c * rows, rows), rows)
        qm[dst, :] = qn[src, :]
        km[dst, :] = kn[src, :]
        vm[dst, :] = v_ref[0, src, :]
        return carry

    lax.fori_loop(0, d1 * (len1 // rows), build_mid, 0)

    def build_top(t, carry):
        per = len2 // rows2
        r, c = t // per, t % per
        src = pl.ds((r % d1) * len1 + r // d1 + r2 * c * rows2, rows2, stride=r2)
        dst = pl.ds(pl.multiple_of(r * len2 + c * rows2, rows2), rows2)
        qs[dst, :] = qm[src, :].astype(_BF16)
        ks[dst, :] = km[src, :].astype(_BF16)
        vs[dst, 0:HEAD_DIM] = vm[src, :].astype(_BF16)
        vs[dst, HEAD_DIM:] = jnp.ones((rows2, HEAD_DIM), _BF16)
        return carry

    lax.fori_loop(0, d2 * (len2 // rows2), build_top, 0)

    def cast_rows(q_src, k_src, v_src):
        def body(t, carry):
            sl = pl.ds(pl.multiple_of(t * rows, rows), rows)
            qs[sl, :] = q_src(sl).astype(_BF16)
            ks[sl, :] = k_src(sl).astype(_BF16)
            vs[sl, 0:HEAD_DIM] = v_src(sl).astype(_BF16)
            return carry
        lax.fori_loop(0, seq // rows, body, 0)

    def branch(window, dil, parent, prev, store):
        sub_len = seq // dil
        half = window // (2 * dil)
        win = min(bq + 2 * half, sub_len)
        per = sub_len // bq
        offsets = {q0 - min(max(q0 - half, 0), sub_len - win) for q0 in range(0, sub_len, bq)}
        assert offsets <= {0, half, 2 * half} and half % 16 == 0 and sub_len % bq == 0

        for v in range(3):
            rel = (lax.broadcasted_iota(jnp.int32, (bq, win), 0)
                   - lax.broadcasted_iota(jnp.int32, (bq, win), 1) + v * half)
            bias[v, :, 0:win] = jnp.where(jnp.abs(rel) <= half, 0.0, NEG_INF)

        def block(t):
            r = t // per
            q0 = (t % per) * bq
            ws = jnp.clip(q0 - half, 0, sub_len - win)
            base = r * sub_len
            own = pl.ds(pl.multiple_of(base + q0, bq), bq)
            keys = pl.ds(pl.multiple_of(base + ws, half), win)
            s = lax.dot_general(qs[own, :], ks[keys, :], (((1,), (1,)), ((), ())),
                                preferred_element_type=_F32) + bias[(q0 - ws) // half, :, 0:win]
            m = jnp.broadcast_to(jnp.max(s, axis=-1, keepdims=True), (bq, HEAD_DIM))
            if prev is not None:
                m_p = prev[1, own, :]
                m = jnp.maximum(m, m_p)
            e = jnp.concatenate([jnp.exp2(s[:, c:c + HEAD_DIM] - m) for c in range(0, win, HEAD_DIM)],
                                axis=1)
            pv = jnp.dot(e.astype(_BF16), vs[keys, :], preferred_element_type=_F32)
            acc, l = pv[:, 0:HEAD_DIM], pv[:, HEAD_DIM:]
            if prev is not None:
                a = jnp.exp2(m_p - m)
                acc = a * prev[0, own, :] + acc
                l = a * prev[2, own, :] + l
            if store is None:
                o_ref[0, own, :] = (acc / l).astype(o_ref.dtype)
            else:
                pd, plen = parent
                ratio = dil // pd
                dst = pl.ds((r % pd) * plen + r // pd + ratio * q0, bq, stride=ratio)
                store[0, dst, :] = acc
                store[1, dst, :] = m
                store[2, dst, :] = l

        def body(tt, carry):
            for g in range(group):
                block(tt * group + g)
            return carry

        lax.fori_loop(0, seq // (bq * group), body, 0)

    branch(w2, d2, (d1, len1), None, run1)
    cast_rows(lambda sl: qm[sl, :], lambda sl: km[sl, :], lambda sl: vm[sl, :])
    branch(w1, d1, (d0, seq), run1, run0)
    cast_rows(lambda sl: qn[sl, :], lambda sl: kn[sl, :], lambda sl: v_ref[0, sl, :])
    branch(w0, d0, None, run0, None)


def _attn_call(proj3, cosf, sinf, a_heads):
    b, s, _ = proj3.shape
    bq = 128
    group = min(32, s // bq)
    assert (s // bq) % group == 0
    half_max = max(w // (2 * d) for w, d in DILATED_PATTERNS)
    col = lambda off: pl.BlockSpec((1, s, HEAD_DIM), lambda bi, h, off=off: (bi, 0, off + h))
    rope_spec = pl.BlockSpec((1, s, HEAD_DIM), lambda bi, h: (bi, 0, 0), pipeline_mode=pl.Buffered(1))
    seq_f32 = pltpu.VMEM((s, HEAD_DIM), _F32)
    seq_bf16 = pltpu.VMEM((s, HEAD_DIM), _BF16)
    return pl.pallas_call(
        functools.partial(_attn_kernel, seq=s, bq=bq, group=group),
        grid=(b, a_heads),
        in_specs=[col(0), col(a_heads), col(2 * a_heads), rope_spec, rope_spec],
        out_specs=pl.BlockSpec((1, s, HEAD_DIM), lambda bi, h: (bi, 0, h)),
        out_shape=jax.ShapeDtypeStruct((b, s, a_heads * HEAD_DIM), _BF16),
        scratch_shapes=[seq_f32] * 5 + [seq_bf16] * 2 + [pltpu.VMEM((s, 2 * HEAD_DIM), _BF16)]
        + [pltpu.VMEM((3, s, HEAD_DIM), _F32)] * 2
        + [pltpu.VMEM((3, bq, bq + 2 * half_max), _F32)],
        compiler_params=_params(("parallel", "parallel")),
    )(proj3, proj3, proj3, cosf, sinf)


def _scan_rows(x, reverse):
    n = x.shape[0]
    row = lax.broadcasted_iota(jnp.int32, x.shape, 0)
    k = 1
    while k < n:
        if reverse:
            x = x + jnp.where(row < n - k, pltpu.roll(x, n - k, axis=0), 0.0)
        else:
            x = x + jnp.where(row >= k, pltpu.roll(x, k, axis=0), 0.0)
        k *= 2
    return x


def _hgrn_chunk(qc, vc16, fp, lb, state_t, reverse):
    c = HGRN_CHUNK
    f = lb + (1.0 - lb) * jax.nn.sigmoid(fp)
    kk = 1.0 - f
    bc = _scan_rows(jnp.log(f), reverse)
    inter = lax.dot_general((qc * jnp.exp(bc)).astype(_BF16), state_t.astype(_BF16),
                            (((1,), (1,)), ((), ())), preferred_element_type=_F32)
    blocks = []
    for i in range(c // HGRN_SUB):
        lo, hi = i * HGRN_SUB, (i + 1) * HGRN_SUB
        beta = bc[hi - 1:hi] if reverse else bc[lo:lo + 1]
        qt = qc[lo:hi] * jnp.exp(bc[lo:hi] - beta)
        kt = kk * jnp.exp(jnp.minimum(beta - bc, HGRN_EXP_CLAMP))
        blocks.append(lax.dot_general(qt.astype(_BF16), kt.astype(_BF16),
                                      (((1,), (1,)), ((), ())), preferred_element_type=_F32))
    scores = jnp.concatenate(blocks, axis=0)
    t_idx = lax.broadcasted_iota(jnp.int32, (c, c), 0)
    s_idx = lax.broadcasted_iota(jnp.int32, (c, c), 1)
    keep = (s_idx >= t_idx) if reverse else (s_idx <= t_idx)
    scores = jnp.where(keep, scores, 0.0)
    intra = jnp.dot(scores.astype(_BF16), vc16, preferred_element_type=_F32)
    last = bc[0:1] if reverse else bc[c - 1:c]
    kd = (kk * jnp.exp(last - bc)).astype(_BF16)
    new_state = jnp.exp(last) * state_t + lax.dot_general(
        vc16, kd, (((0,), (0,)), ((), ())), preferred_element_type=_F32)
    return inter + intra, new_state


def _hgrn_kernel(q_ref, i_ref, g_ref, ff_ref, fb_ref, gf_ref, gb_ref, ng_ref, o_ref, yf, yb, *, seq):
    c = HGRN_CHUNK
    n = seq // c

    def lower_bound(gam_ref):
        gam = gam_ref[...]
        e = jnp.exp(gam - jnp.max(gam, axis=0, keepdims=True))
        return e[0:1] / jnp.sum(e, axis=0, keepdims=True)

    lb_f = lower_bound(gf_ref)
    lb_b = lower_bound(gb_ref)

    def step(t, carry):
        st_f, st_b = carry
        sl_f = pl.ds(pl.multiple_of(t * c, c), c)
        sl_b = pl.ds(pl.multiple_of((n - 1 - t) * c, c), c)
        y, st_f = _hgrn_chunk(jax.nn.silu(q_ref[0, sl_f, :]), i_ref[0, sl_f, :].astype(_BF16),
                              ff_ref[0, sl_f, :], lb_f, st_f, False)
        yf[sl_f, :] = y
        y, st_b = _hgrn_chunk(jax.nn.silu(q_ref[0, sl_b, :]), i_ref[0, sl_b, :].astype(_BF16),
                              fb_ref[0, sl_b, :], lb_b, st_b, True)
        yb[sl_b, :] = y
        return st_f, st_b

    zero = jnp.zeros((HEAD_DIM, HEAD_DIM), _F32)
    lax.fori_loop(0, n, step, (zero, zero), unroll=8)

    rows = 256

    def finish(t, carry):
        sl = pl.ds(pl.multiple_of(t * rows, rows), rows)
        y = yf[sl, :] + yb[sl, :]
        y = y * lax.rsqrt(jnp.mean(jnp.square(y), axis=-1, keepdims=True) + RMS_EPS)
        o_ref[0, sl, :] = (y * ng_ref[...] * jax.nn.silu(g_ref[0, sl, :])).astype(o_ref.dtype)
        return carry

    lax.fori_loop(0, seq // rows, finish, 0)


def _hgrn_call(proj3, gamma_f, gamma_b, norm_g, col0, b_heads):
    b, s, _ = proj3.shape
    col = lambda off: pl.BlockSpec((1, s, HEAD_DIM), lambda bi, h, off=off: (bi, 0, col0 + off + h))
    gam_spec = pl.BlockSpec((gamma_f.shape[0], HEAD_DIM), lambda bi, h: (0, h))
    return pl.pallas_call(
        functools.partial(_hgrn_kernel, seq=s),
        grid=(b, b_heads),
        in_specs=[col(0), col(b_heads), col(2 * b_heads), col(3 * b_heads), col(4 * b_heads),
                  gam_spec, gam_spec, pl.BlockSpec((1, HEAD_DIM), lambda bi, h: (0, h))],
        out_specs=pl.BlockSpec((1, s, HEAD_DIM), lambda bi, h: (bi, 0, h)),
        out_shape=jax.ShapeDtypeStruct((b, s, b_heads * HEAD_DIM), _BF16),
        scratch_shapes=[pltpu.VMEM((s, HEAD_DIM), _F32)] * 2,
        compiler_params=_params(("parallel", "parallel")),
    )(proj3, proj3, proj3, proj3, proj3, gamma_f, gamma_b, norm_g.reshape(1, -1))


def _memattn_kernel(q_ref, kv_ref, o_ref, *, m_width):
    hd = m_width // M_HEADS
    scale = hd ** -0.5
    for h in range(M_HEADS):
        q = (q_ref[0, :, h * hd:(h + 1) * hd] * scale).astype(_BF16)
        k = kv_ref[0, :, h * hd:(h + 1) * hd].astype(_BF16)
        v = kv_ref[0, :, m_width + h * hd:m_width + (h + 1) * hd].astype(_BF16)
        s = lax.dot_general(q, k, (((1,), (1,)), ((), ())), preferred_element_type=_F32)
        e = jnp.exp(s - jnp.max(s, axis=-1, keepdims=True))
        o = jnp.dot(e.astype(_BF16), v, preferred_element_type=_F32) / jnp.sum(e, axis=-1, keepdims=True)
        o_ref[0, :, h * hd:(h + 1) * hd] = o.astype(o_ref.dtype)


def _memattn_call(proj3, kvm, q_col_block, m_width):
    b, s, _ = proj3.shape
    mlen = kvm.shape[1]
    tq = _pick(s, (512, 256, 128))
    return pl.pallas_call(
        functools.partial(_memattn_kernel, m_width=m_width),
        grid=(b, s // tq),
        in_specs=[pl.BlockSpec((1, tq, m_width), lambda bi, i: (bi, i, q_col_block)),
                  pl.BlockSpec((1, mlen, 2 * m_width), lambda bi, i: (bi, 0, 0))],
        out_specs=pl.BlockSpec((1, tq, m_width), lambda bi, i: (bi, i, 0)),
        out_shape=jax.ShapeDtypeStruct((b, s, m_width), _BF16),
        compiler_params=_params(("parallel", "parallel")),
    )(proj3, kvm)


def _mix_out_kernel(a_ref, b_ref, m_ref, wa_ref, wb_ref, wm_ref, r_ref, o_ref, *, alpha):
    acc = jnp.dot(a_ref[...], wa_ref[...].astype(_BF16), preferred_element_type=_F32)
    acc += jnp.dot(b_ref[...], wb_ref[...].astype(_BF16), preferred_element_type=_F32)
    acc += jnp.dot(m_ref[...], wm_ref[...].astype(_BF16), preferred_element_type=_F32)
    o_ref[...] = alpha * r_ref[...] + acc


def _mix_out_call(oa, ob, om, w_o, resid, alpha):
    m, aw = oa.shape
    bw, mw = ob.shape[1], om.shape[1]
    n = w_o.shape[1]
    tm = _pick(m, (1024, 512, 256))
    tn = _pick(n, (512, 256, 128))
    assert aw == bw and (aw + bw) % mw == 0
    return pl.pallas_call(
        functools.partial(_mix_out_kernel, alpha=alpha),
        grid=(m // tm, n // tn),
        in_specs=[pl.BlockSpec((tm, aw), lambda i, j: (i, 0)),
                  pl.BlockSpec((tm, bw), lambda i, j: (i, 0)),
                  pl.BlockSpec((tm, mw), lambda i, j: (i, 0)),
                  pl.BlockSpec((aw, tn), lambda i, j: (0, j)),
                  pl.BlockSpec((bw, tn), lambda i, j: (1, j)),
                  pl.BlockSpec((mw, tn), lambda i, j: ((aw + bw) // mw, j)),
                  pl.BlockSpec((tm, tn), lambda i, j: (i, j))],
        out_specs=pl.BlockSpec((tm, tn), lambda i, j: (i, j)),
        out_shape=jax.ShapeDtypeStruct((m, n), _F32),
        compiler_params=_params(("parallel", "parallel")),
    )(oa, ob, om, w_o, w_o, w_o, resid)


def _ln_kernel(*refs, alpha):
    if alpha is None:
        y_ref, g_ref, b_ref, *o_refs = refs
        y = y_ref[...]
    else:
        y_ref, r_ref, g_ref, b_ref, *o_refs = refs
        y = alpha * r_ref[...] + y_ref[...]
    mu = jnp.mean(y, axis=-1, keepdims=True)
    d = y - mu
    var = jnp.mean(d * d, axis=-1, keepdims=True)
    out = d * lax.rsqrt(var + LN_EPS) * g_ref[...] + b_ref[...]
    for o_ref in o_refs:
        o_ref[...] = out.astype(o_ref.dtype)


def _layernorm_call(y, g, b, out_dtypes, resid=None, alpha=None):
    m, n = y.shape
    tr = _pick(m, (256, 128, 64))
    row = pl.BlockSpec((tr, n), lambda i: (i, 0))
    vec = pl.BlockSpec((1, n), lambda i: (0, 0))
    rows_in = (y,) if resid is None else (y, resid)
    return pl.pallas_call(
        functools.partial(_ln_kernel, alpha=None if resid is None else alpha),
        grid=(m // tr,),
        in_specs=[row] * len(rows_in) + [vec, vec],
        out_specs=[row] * len(out_dtypes),
        out_shape=[jax.ShapeDtypeStruct((m, n), dt) for dt in out_dtypes],
        compiler_params=_params(("parallel",)),
    )(*rows_in, g.reshape(1, n), b.reshape(1, n))


def _ffn_up_kernel(x_hbm, wg_ref, wu_ref, cw_ref, cb_ref, h_ref, lhs_ref, sem, *, tm, tiles_per_seq):
    i = pl.program_id(0)
    j = pl.program_id(1)

    @pl.when(j == 0)
    def _():
        pos = i % tiles_per_seq
        row0 = pl.multiple_of(i * tm, tm)
        main = pltpu.make_async_copy(x_hbm.at[pl.ds(row0, tm), :], lhs_ref.at[pl.ds(HALO, tm), :],
                                     sem.at[0])
        main.start()

        @pl.when(pos > 0)
        def _():
            cp = pltpu.make_async_copy(x_hbm.at[pl.ds(row0 - HALO, HALO), :],
                                       lhs_ref.at[pl.ds(0, HALO), :], sem.at[1])
            cp.start()
            cp.wait()

        @pl.when(pos == 0)
        def _():
            lhs_ref[0:HALO, :] = jnp.zeros((HALO, lhs_ref.shape[1]), lhs_ref.dtype)

        @pl.when(pos < tiles_per_seq - 1)
        def _():
            cp = pltpu.make_async_copy(x_hbm.at[pl.ds(row0 + tm, HALO), :],
                                       lhs_ref.at[pl.ds(HALO + tm, HALO), :], sem.at[2])
            cp.start()
            cp.wait()

        @pl.when(pos == tiles_per_seq - 1)
        def _():
            lhs_ref[HALO + tm:, :] = jnp.zeros((HALO, lhs_ref.shape[1]), lhs_ref.dtype)

        main.wait()

    ext = tm + 2 * HALO
    g = jnp.dot(lhs_ref[...], wg_ref[...].astype(_BF16), preferred_element_type=_F32)
    u = jnp.dot(lhs_ref[HALO:HALO + tm, :], wu_ref[...].astype(_BF16), preferred_element_type=_F32)
    g_prev = pltpu.roll(g, 1, axis=0)[HALO:HALO + tm]
    g_next = pltpu.roll(g, ext - 1, axis=0)[HALO:HALO + tm]
    gc = cb_ref[...] + g_prev * cw_ref[0:1, :] + g[HALO:HALO + tm] * cw_ref[1:2, :] + g_next * cw_ref[2:3, :]
    h_ref[...] = (jax.nn.silu(gc) * u).astype(h_ref.dtype)


def _ffn_up_call(x1b, w_up, conv_w, conv_b, seq):
    m, d = x1b.shape
    d_ff = w_up.shape[1] // 2
    tm = _pick(seq, (2048, 1024, 512, 256))
    tn = _pick(d_ff, (256, 128))
    nj = d_ff // tn
    return pl.pallas_call(
        functools.partial(_ffn_up_kernel, tm=tm, tiles_per_seq=seq // tm),
        grid=(m // tm, nj),
        in_specs=[pl.BlockSpec(memory_space=pl.ANY),
                  pl.BlockSpec((d, tn), lambda i, j: (0, j)),
                  pl.BlockSpec((d, tn), lambda i, j: (0, j + nj)),
                  pl.BlockSpec((CONV_WIDTH, tn), lambda i, j: (0, j)),
                  pl.BlockSpec((1, tn), lambda i, j: (0, j))],
        out_specs=pl.BlockSpec((tm, tn), lambda i, j: (i, j)),
        out_shape=jax.ShapeDtypeStruct((m, d_ff), _BF16),
        scratch_shapes=[pltpu.VMEM((tm + 2 * HALO, d), _BF16), pltpu.SemaphoreType.DMA((3,))],
        compiler_params=_params(("arbitrary", "arbitrary")),
    )(x1b, w_up, w_up, conv_w, conv_b.reshape(1, d_ff))


def _rope_tables(positions):
    half = HEAD_DIM // 2
    inv = 1.0 / (ROPE_THETA ** (jnp.arange(0, HEAD_DIM, 2, dtype=_F32) / HEAD_DIM))
    ang = positions.astype(_F32)[..., None] * inv
    cos, sin = jnp.cos(ang), jnp.sin(ang)
    return jnp.concatenate([cos, cos], axis=-1), jnp.concatenate([-sin, sin], axis=-1)


def kernel(x, mem, positions, w_in, w_mem_kv, w_o, hgrn_gamma_fwd, hgrn_gamma_bwd, hgrn_norm_g,
           ln1_g, ln1_b, w_up, conv_w, conv_b, w_down, ln2_g, ln2_b):
    b, s, d = x.shape
    depth = w_in.shape[0]
    a_width = b_width = 3 * d // 8
    m_width = d // 4
    a_heads = a_width // HEAD_DIM
    b_heads = b_width // HEAD_DIM
    alpha = (2 * depth) ** 0.25
    m = b * s
    cosf, sinf = _rope_tables(positions)
    tm = _pick(m, (1024, 512, 256))

    xf = x.reshape(m, d)
    for layer in range(depth):
        assert layer == 0, "lower-bound running sum is implemented for the first layer only"
        xb = xf.astype(_BF16)
        proj = _matmul_rows_resident(xb, w_in[layer], _F32, _pick(m, (2048, 1024, 512, 256)),
                                     _pick(w_in.shape[2], (512, 256, 128)))
        proj3 = proj.reshape(b, s, -1)
        o_a = _attn_call(proj3, cosf, sinf, a_heads)
        o_b = _hgrn_call(proj3, hgrn_gamma_fwd, hgrn_gamma_bwd, hgrn_norm_g[layer],
                         3 * a_heads, b_heads)
        kvm = _matmul(mem.reshape(-1, d), w_mem_kv[layer], _F32,
                      _pick(b * mem.shape[1], (256, 128)), _pick(2 * m_width, (512, 256, 128)))
        o_m = _memattn_call(proj3, kvm.reshape(b, mem.shape[1], 2 * m_width),
                            (3 * a_width + 5 * b_width) // m_width, m_width)
        y1 = _mix_out_call(o_a.reshape(m, a_width), o_b.reshape(m, b_width), o_m.reshape(m, m_width),
                           w_o[layer], xf, alpha)
        x1, x1b = _layernorm_call(y1, ln1_g[layer], ln1_b[layer], (_F32, _BF16))
        h = _ffn_up_call(x1b, w_up[layer], conv_w[layer], conv_b[layer], s)
        ffn = _matmul_ktiled(h, w_down[layer], _pick(m, (2048, 1024, 512, 256)),
                             _pick(d, (1024, 512, 256)), 1024)
        (xf,) = _layernorm_call(ffn, ln2_g[layer], ln2_b[layer], (_F32,), resid=x1, alpha=alpha)
    return xf.reshape(b, s, d)
```

```python
import functools

import jax
import jax.numpy as jnp
from jax import lax
from jax.experimental import pallas as pl
from jax.experimental.pallas import tpu as pltpu

HEAD_DIM = 128
DILATED_PATTERNS = ((128, 1), (512, 4), (2048, 16))
HGRN_CHUNK = 64
HGRN_SUB = 16
HGRN_EXP_CLAMP = 60.0
M_HEADS = 4
CONV_WIDTH = 3
ROPE_THETA = 10000.0
LN_EPS = 1e-5
RMS_EPS = 1e-6
NEG_INF = -1e30
LOG2_E = 1.4426950408889634
HALO = 16
VMEM_LIMIT = 56 * 1024 * 1024

_BF16 = jnp.bfloat16
_F32 = jnp.float32


def _pick(n, prefs):
    for p in prefs:
        if n % p == 0:
            return p
    return n


def _params(sem):
    return pltpu.CompilerParams(dimension_semantics=sem, vmem_limit_bytes=VMEM_LIMIT)


def _mm_kernel(x_ref, w_ref, o_ref):
    o_ref[...] = jnp.dot(x_ref[...].astype(_BF16), w_ref[...].astype(_BF16),
                         preferred_element_type=_F32).astype(o_ref.dtype)


def _mm_acc_kernel(x_ref, w_ref, o_ref, acc_ref, *, nk, tail):
    k = pl.program_id(2)
    tk = x_ref.shape[1]

    def part(valid):
        x, w = x_ref[...], w_ref[...]
        if valid < tk:
            col = lax.broadcasted_iota(jnp.int32, x.shape, 1)
            x = jnp.where(col < valid, x.astype(_F32), 0.0)
            row = lax.broadcasted_iota(jnp.int32, w.shape, 0)
            w = jnp.where(row < valid, w, 0.0)
        return jnp.dot(x.astype(_BF16), w.astype(_BF16), preferred_element_type=_F32)

    @pl.when(k == 0)
    def _():
        acc_ref[...] = part(tk)

    @pl.when(jnp.logical_and(k > 0, k < nk - 1))
    def _():
        acc_ref[...] += part(tk)

    @pl.when(k == nk - 1)
    def _():
        o_ref[...] = acc_ref[...] + part(tail)


def _matmul(x, w, out_dtype, tm, tn):
    m, kdim = x.shape
    n = w.shape[1]
    return pl.pallas_call(
        _mm_kernel,
        grid=(m // tm, n // tn),
        in_specs=[pl.BlockSpec((tm, kdim), lambda i, j: (i, 0)),
                  pl.BlockSpec((kdim, tn), lambda i, j: (0, j))],
        out_specs=pl.BlockSpec((tm, tn), lambda i, j: (i, j)),
        out_shape=jax.ShapeDtypeStruct((m, n), out_dtype),
        compiler_params=_params(("parallel", "parallel")),
    )(x, w)


def _mm_rows_kernel(x_hbm, w_ref, o_ref, x_scr, sem, *, tm):
    i = pl.program_id(0)

    @pl.when(pl.program_id(1) == 0)
    def _():
        cp = pltpu.make_async_copy(x_hbm.at[pl.ds(pl.multiple_of(i * tm, tm), tm), :], x_scr, sem.at[0])
        cp.start()
        cp.wait()

    o_ref[...] = jnp.dot(x_scr[...], w_ref[...].astype(_BF16),
                         preferred_element_type=_F32).astype(o_ref.dtype)


def _matmul_rows_resident(x, w, out_dtype, tm, tn):
    m, kdim = x.shape
    n = w.shape[1]
    return pl.pallas_call(
        functools.partial(_mm_rows_kernel, tm=tm),
        grid=(m // tm, n // tn),
        in_specs=[pl.BlockSpec(memory_space=pl.ANY),
                  pl.BlockSpec((kdim, tn), lambda i, j: (0, j))],
        out_specs=pl.BlockSpec((tm, tn), lambda i, j: (i, j)),
        out_shape=jax.ShapeDtypeStruct((m, n), out_dtype),
        scratch_shapes=[pltpu.VMEM((tm, kdim), x.dtype), pltpu.SemaphoreType.DMA((1,))],
        compiler_params=_params(("arbitrary", "arbitrary")),
    )(x, w)


def _matmul_ktiled(x, w, tm, tn, tk):
    m, kdim = x.shape
    n = w.shape[1]
    nk = pl.cdiv(kdim, tk)
    assert nk >= 2
    return pl.pallas_call(
        functools.partial(_mm_acc_kernel, nk=nk, tail=kdim - (nk - 1) * tk),
        grid=(m // tm, n // tn, nk),
        in_specs=[pl.BlockSpec((tm, tk), lambda i, j, k: (i, k)),
                  pl.BlockSpec((tk, tn), lambda i, j, k: (k, j))],
        out_specs=pl.BlockSpec((tm, tn), lambda i, j, k: (i, j)),
        out_shape=jax.ShapeDtypeStruct((m, n), _F32),
        scratch_shapes=[pltpu.VMEM((tm, tn), _F32)],
        compiler_params=_params(("parallel", "parallel", "arbitrary")),
    )(x, w)


def _attn_kernel(q_ref, k_ref, v_ref, cos_ref, sin_ref, o_ref, qn, kn, qm, km, vm, qs, ks, vs,
                 run0, run1, bias, *, seq, bq, group):
    scale = HEAD_DIM ** -0.5 * LOG2_E
    (w0, d0), (w1, d1), (w2, d2) = sorted(DILATED_PATTERNS, key=lambda wd: wd[1])
    assert d0 == 1 and d1 % d0 == 0 and d2 % d1 == 0
    r1, r2 = d1 // d0, d2 // d1
    len1, len2 = seq // d1, seq // d2
    rows = 256
    rows2 = min(rows, len2)

    def rope_rows(t, carry):
        sl = pl.ds(pl.multiple_of(t * rows, rows), rows)
        cos, sin = cos_ref[0, sl, :], sin_ref[0, sl, :]
        qv, kv = q_ref[0, sl, :], k_ref[0, sl, :]
        qn[sl, :] = (qv * cos + pltpu.roll(qv, HEAD_DIM // 2, axis=1) * sin) * scale
        kn[sl, :] = kv * cos + pltpu.roll(kv, HEAD_DIM // 2, axis=1) * sin
        return carry

    lax.fori_loop(0, seq // rows, rope_rows, 0)

    def build_mid(t, carry):
        per = len1 // rows
        r, c = t // per, t % per
        src = pl.ds(r + r1 * c * rows, rows, stride=r1)
        dst = pl.ds(pl.multiple_of(r * len1 + c * rows, rows), rows)
        qm[dst, :] = qn[src, :]
        km[dst, :] = kn[src, :]
        vm[dst, :] = v_ref[0, src, :]
        return carry

    lax.fori_loop(0, d1 * (len1 // rows), build_mid, 0)

    def build_top(t, carry):
        per = len2 // rows2
        r, c = t // per, t % per
        src = pl.ds((r % d1) * len1 + r // d1 + r2 * c * rows2, rows2, stride=r2)
        dst = pl.ds(pl.multiple_of(r * len2 + c * rows2, rows2), rows2)
        qs[dst, :] = qm[src, :].astype(_BF16)
        ks[dst, :] = km[src, :].astype(_BF16)
        vs[dst, 0:HEAD_DIM] = vm[src, :].astype(_BF16)
        vs[dst, HEAD_DIM:] = jnp.ones((rows2, HEAD_DIM), _BF16)
        return carry

    lax.fori_loop(0, d2 * (len2 // rows2), build_top, 0)

    def cast_rows(q_src, k_src, v_src):
        def body(t, carry):
            sl = pl.ds(pl.multiple_of(t * rows, rows), rows)
            qs[sl, :] = q_src(sl).astype(_BF16)
            ks[sl, :] = k_src(sl).astype(_BF16)
            vs[sl, 0:HEAD_DIM] = v_src(sl).astype(_BF16)
            return carry
        lax.fori_loop(0, seq // rows, body, 0)

    def branch(window, dil, parent, prev, store):
        sub_len = seq // dil
        half = window // (2 * dil)
        win = min(bq + 2 * half, sub_len)
        per = sub_len // bq
        offsets = {q0 - min(max(q0 - half, 0), sub_len - win) for q0 in range(0, sub_len, bq)}
        assert offsets <= {0, half, 2 * half} and half % 16 == 0 and sub_len % bq == 0

        for v in range(3):
            rel = (lax.broadcasted_iota(jnp.int32, (bq, win), 0)
                   - lax.broadcasted_iota(jnp.int32, (bq, win), 1) + v * half)
            bias[v, :, 0:win] = jnp.where(jnp.abs(rel) <= half, 0.0, NEG_INF)

        def block(t):
            r = t // per
            q0 = (t % per) * bq
            ws = jnp.clip(q0 - half, 0, sub_len - win)
            base = r * sub_len
            own = pl.ds(pl.multiple_of(base + q0, bq), bq)
            keys = pl.ds(pl.multiple_of(base + ws, half), win)
            s = lax.dot_general(qs[own, :], ks[keys, :], (((1,), (1,)), ((), ())),
                                preferred_element_type=_F32) + bias[(q0 - ws) // half, :, 0:win]
            m = jnp.broadcast_to(jnp.max(s, axis=-1, keepdims=True), (bq, HEAD_DIM))
            if prev is not None:
                m_p = prev[1, own, :]
                m = jnp.maximum(m, m_p)
            e = jnp.concatenate([jnp.exp2(s[:, c:c + HEAD_DIM] - m) for c in range(0, win, HEAD_DIM)],
                                axis=1)
            pv = jnp.dot(e.astype(_BF16), vs[keys, :], preferred_element_type=_F32)
            acc, l = pv[:, 0:HEAD_DIM], pv[:, HEAD_DIM:]
            if prev is not None:
                a = jnp.exp2(m_p - m)
                acc = a * prev[0, own, :] + acc
                l = a * prev[2, own, :] + l
            if store is None:
                o_ref[0, own, :] = (acc / l).astype(o_ref.dtype)
            else:
                pd, plen = parent
                ratio = dil // pd
                dst = pl.ds((r % pd) * plen + r // pd + ratio * q0, bq, stride=ratio)
                store[0, dst, :] = acc
                store[1, dst, :] = m
                store[2, dst, :] = l

        def body(tt, carry):
            for g in range(group):
                block(tt * group + g)
            return carry

        lax.fori_loop(0, seq // (bq * group), body, 0)

    branch(w2, d2, (d1, len1), None, run1)
    cast_rows(lambda sl: qm[sl, :], lambda sl: km[sl, :], lambda sl: vm[sl, :])
    branch(w1, d1, (d0, seq), run1, run0)
    cast_rows(lambda sl: qn[sl, :], lambda sl: kn[sl, :], lambda sl: v_ref[0, sl, :])
    branch(w0, d0, None, run0, None)


def _attn_call(proj3, cosf, sinf, a_heads):
    b, s, _ = proj3.shape
    bq = 128
    group = min(32, s // bq)
    assert (s // bq) % group == 0
    half_max = max(w // (2 * d) for w, d in DILATED_PATTERNS)
    col = lambda off: pl.BlockSpec((1, s, HEAD_DIM), lambda bi, h, off=off: (bi, 0, off + h))
    rope_spec = pl.BlockSpec((1, s, HEAD_DIM), lambda bi, h: (bi, 0, 0), pipeline_mode=pl.Buffered(1))
    seq_f32 = pltpu.VMEM((s, HEAD_DIM), _F32)
    seq_bf16 = pltpu.VMEM((s, HEAD_DIM), _BF16)
    return pl.pallas_call(
        functools.partial(_attn_kernel, seq=s, bq=bq, group=group),
        grid=(b, a_heads),
        in_specs=[col(0), col(a_heads), col(2 * a_heads), rope_spec, rope_spec],
        out_specs=pl.BlockSpec((1, s, HEAD_DIM), lambda bi, h: (bi, 0, h)),
        out_shape=jax.ShapeDtypeStruct((b, s, a_heads * HEAD_DIM), _BF16),
        scratch_shapes=[seq_f32] * 5 + [seq_bf16] * 2 + [pltpu.VMEM((s, 2 * HEAD_DIM), _BF16)]
        + [pltpu.VMEM((3, s, HEAD_DIM), _F32)] * 2
        + [pltpu.VMEM((3, bq, bq + 2 * half_max), _F32)],
        compiler_params=_params(("parallel", "parallel")),
    )(proj3, proj3, proj3, cosf, sinf)


def _scan_rows(x, reverse):
    n = x.shape[0]
    row = lax.broadcasted_iota(jnp.int32, x.shape, 0)
    k = 1
    while k < n:
        if reverse:
            x = x + jnp.where(row < n - k, pltpu.roll(x, n - k, axis=0), 0.0)
        else:
            x = x + jnp.where(row >= k, pltpu.roll(x, k, axis=0), 0.0)
        k *= 2
    return x


def _hgrn_chunk(qc, vc16, fp, lb, state_t, reverse):
    c = HGRN_CHUNK
    f = lb + (1.0 - lb) * jax.nn.sigmoid(fp)
    kk = 1.0 - f
    bc = _scan_rows(jnp.log(f), reverse)
    inter = lax.dot_general((qc * jnp.exp(bc)).astype(_BF16), state_t.astype(_BF16),
                            (((1,), (1,)), ((), ())), preferred_element_type=_F32)
    blocks = []
    for i in range(c // HGRN_SUB):
        lo, hi = i * HGRN_SUB, (i + 1) * HGRN_SUB
        beta = bc[hi - 1:hi] if reverse else bc[lo:lo + 1]
        qt = qc[lo:hi] * jnp.exp(bc[lo:hi] - beta)
        kt = kk * jnp.exp(jnp.minimum(beta - bc, HGRN_EXP_CLAMP))
        blocks.append(lax.dot_general(qt.astype(_BF16), kt.astype(_BF16),
                                      (((1,), (1,)), ((), ())), preferred_element_type=_F32))
    scores = jnp.concatenate(blocks, axis=0)
    t_idx = lax.broadcasted_iota(jnp.int32, (c, c), 0)
    s_idx = lax.broadcasted_iota(jnp.int32, (c, c), 1)
    keep = (s_idx >= t_idx) if reverse else (s_idx <= t_idx)
    scores = jnp.where(keep, scores, 0.0)
    intra = jnp.dot(scores.astype(_BF16), vc16, preferred_element_type=_F32)
    last = bc[0:1] if reverse else bc[c - 1:c]
    kd = (kk * jnp.exp(last - bc)).astype(_BF16)
    new_state = jnp.exp(last) * state_t + lax.dot_general(
        vc16, kd, (((0,), (0,)), ((), ())), preferred_element_type=_F32)
    return inter + intra, new_state


def _hgrn_kernel(q_ref, i_ref, g_ref, ff_ref, fb_ref, gf_ref, gb_ref, ng_ref, o_ref, yf, yb, *, seq):
    c = HGRN_CHUNK
    n = seq // c

    def lower_bound(gam_ref):
        gam = gam_ref[...]
        e = jnp.exp(gam - jnp.max(gam, axis=0, keepdims=True))
        return e[0:1] / jnp.sum(e, axis=0, keepdims=True)

    lb_f = lower_bound(gf_ref)
    lb_b = lower_bound(gb_ref)

    def step(t, carry):
        st_f, st_b = carry
        sl_f = pl.ds(pl.multiple_of(t * c, c), c)
        sl_b = pl.ds(pl.multiple_of((n - 1 - t) * c, c), c)
        y, st_f = _hgrn_chunk(jax.nn.silu(q_ref[0, sl_f, :]), i_ref[0, sl_f, :].astype(_BF16),
                              ff_ref[0, sl_f, :], lb_f, st_f, False)
        yf[sl_f, :] = y
        y, st_b = _hgrn_chunk(jax.nn.silu(q_ref[0, sl_b, :]), i_ref[0, sl_b, :].astype(_BF16),
                              fb_ref[0, sl_b, :], lb_b, st_b, True)
        yb[sl_b, :] = y
        return st_f, st_b

    zero = jnp.zeros((HEAD_DIM, HEAD_DIM), _F32)
    lax.fori_loop(0, n, step, (zero, zero), unroll=8)

    rows = 256

    def finish(t, carry):
        sl = pl.ds(pl.multiple_of(t * rows, rows), rows)
        y = yf[sl, :] + yb[sl, :]
        y = y * lax.rsqrt(jnp.mean(jnp.square(y), axis=-1, keepdims=True) + RMS_EPS)
        o_ref[0, sl, :] = (y * ng_ref[...] * jax.nn.silu(g_ref[0, sl, :])).astype(o_ref.dtype)
        return carry

    lax.fori_loop(0, seq // rows, finish, 0)


def _hgrn_call(proj3, gamma_f, gamma_b, norm_g, col0, b_heads):
    b, s, _ = proj3.shape
    col = lambda off: pl.BlockSpec((1, s, HEAD_DIM), lambda bi, h, off=off: (bi, 0, col0 + off + h))
    gam_spec = pl.BlockSpec((gamma_f.shape[0], HEAD_DIM), lambda bi, h: (0, h))
    return pl.pallas_call(
        functools.partial(_hgrn_kernel, seq=s),
        grid=(b, b_heads),
        in_specs=[col(0), col(b_heads), col(2 * b_heads), col(3 * b_heads), col(4 * b_heads),
                  gam_spec, gam_spec, pl.BlockSpec((1, HEAD_DIM), lambda bi, h: (0, h))],
        out_specs=pl.BlockSpec((1, s, HEAD_DIM), lambda bi, h: (bi, 0, h)),
        out_shape=jax.ShapeDtypeStruct((b, s, b_heads * HEAD_DIM), _BF16),
        scratch_shapes=[pltpu.VMEM((s, HEAD_DIM), _F32)] * 2,
        compiler_params=_params(("parallel", "parallel")),
    )(proj3, proj3, proj3, proj3, proj3, gamma_f, gamma_b, norm_g.reshape(1, -1))


def _memattn_kernel(q_ref, kv_ref, o_ref, *, m_width):
    hd = m_width // M_HEADS
    scale = hd ** -0.5
    for h in range(M_HEADS):
        q = (q_ref[0, :, h * hd:(h + 1) * hd] * scale).astype(_BF16)
        k = kv_ref[0, :, h * hd:(h + 1) * hd].astype(_BF16)
        v = kv_ref[0, :, m_width + h * hd:m_width + (h + 1) * hd].astype(_BF16)
        s = lax.dot_general(q, k, (((1,), (1,)), ((), ())), preferred_element_type=_F32)
        e = jnp.exp(s - jnp.max(s, axis=-1, keepdims=True))
        o = jnp.dot(e.astype(_BF16), v, preferred_element_type=_F32) / jnp.sum(e, axis=-1, keepdims=True)
        o_ref[0, :, h * hd:(h + 1) * hd] = o.astype(o_ref.dtype)


def _memattn_call(proj3, kvm, q_col_block, m_width):
    b, s, _ = proj3.shape
    mlen = kvm.shape[1]
    tq = _pick(s, (512, 256, 128))
    return pl.pallas_call(
        functools.partial(_memattn_kernel, m_width=m_width),
        grid=(b, s // tq),
        in_specs=[pl.BlockSpec((1, tq, m_width), lambda bi, i: (bi, i, q_col_block)),
                  pl.BlockSpec((1, mlen, 2 * m_width), lambda bi, i: (bi, 0, 0))],
        out_specs=pl.BlockSpec((1, tq, m_width), lambda bi, i: (bi, i, 0)),
        out_shape=jax.ShapeDtypeStruct((b, s, m_width), _BF16),
        compiler_params=_params(("parallel", "parallel")),
    )(proj3, kvm)


def _mix_out_kernel(a_ref, b_ref, m_ref, wa_ref, wb_ref, wm_ref, r_ref, o_ref, *, alpha):
    acc = jnp.dot(a_ref[...], wa_ref[...].astype(_BF16), preferred_element_type=_F32)
    acc += jnp.dot(b_ref[...], wb_ref[...].astype(_BF16), preferred_element_type=_F32)
    acc += jnp.dot(m_ref[...], wm_ref[...].astype(_BF16), preferred_element_type=_F32)
    o_ref[...] = alpha * r_ref[...] + acc


def _mix_out_call(oa, ob, om, w_o, resid, alpha):
    m, aw = oa.shape
    bw, mw = ob.shape[1], om.shape[1]
    n = w_o.shape[1]
    tm = _pick(m, (1024, 512, 256))
    tn = _pick(n, (512, 256, 128))
    assert aw == bw and (aw + bw) % mw == 0
    return pl.pallas_call(
        functools.partial(_mix_out_kernel, alpha=alpha),
        grid=(m // tm, n // tn),
        in_specs=[pl.BlockSpec((tm, aw), lambda i, j: (i, 0)),
                  pl.BlockSpec((tm, bw), lambda i, j: (i, 0)),
                  pl.BlockSpec((tm, mw), lambda i, j: (i, 0)),
                  pl.BlockSpec((aw, tn), lambda i, j: (0, j)),
                  pl.BlockSpec((bw, tn), lambda i, j: (1, j)),
                  pl.BlockSpec((mw, tn), lambda i, j: ((aw + bw) // mw, j)),
                  pl.BlockSpec((tm, tn), lambda i, j: (i, j))],
        out_specs=pl.BlockSpec((tm, tn), lambda i, j: (i, j)),
        out_shape=jax.ShapeDtypeStruct((m, n), _F32),
        compiler_params=_params(("parallel", "parallel")),
    )(oa, ob, om, w_o, w_o, w_o, resid)


def _ln_kernel(*refs, alpha):
    if alpha is None:
        y_ref, g_ref, b_ref, *o_refs = refs
        y = y_ref[...]
    else:
        y_ref, r_ref, g_ref, b_ref, *o_refs = refs
        y = alpha * r_ref[...] + y_ref[...]
    mu = jnp.mean(y, axis=-1, keepdims=True)
    d = y - mu
    var = jnp.mean(d * d, axis=-1, keepdims=True)
    out = d * lax.rsqrt(var + LN_EPS) * g_ref[...] + b_ref[...]
    for o_ref in o_refs:
        o_ref[...] = out.astype(o_ref.dtype)


def _layernorm_call(y, g, b, out_dtypes, resid=None, alpha=None):
    m, n = y.shape
    tr = _pick(m, (256, 128, 64))
    row = pl.BlockSpec((tr, n), lambda i: (i, 0))
    vec = pl.BlockSpec((1, n), lambda i: (0, 0))
    rows_in = (y,) if resid is None else (y, resid)
    return pl.pallas_call(
        functools.partial(_ln_kernel, alpha=None if resid is None else alpha),
        grid=(m // tr,),
        in_specs=[row] * len(rows_in) + [vec, vec],
        out_specs=[row] * len(out_dtypes),
        out_shape=[jax.ShapeDtypeStruct((m, n), dt) for dt in out_dtypes],
        compiler_params=_params(("parallel",)),
    )(*rows_in, g.reshape(1, n), b.reshape(1, n))


def _ffn_up_kernel(x_hbm, wg_ref, wu_ref, cw_ref, cb_ref, h_ref, lhs_ref, sem, *, tm, tiles_per_seq):
    i = pl.program_id(0)
    j = pl.program_id(1)

    @pl.when(j == 0)
    def _():
        pos = i % tiles_per_seq
        row0 = pl.multiple_of(i * tm, tm)
        main = pltpu.make_async_copy(x_hbm.at[pl.ds(row0, tm), :], lhs_ref.at[pl.ds(HALO, tm), :],
                                     sem.at[0])
        main.start()

        @pl.when(pos > 0)
        def _():
            cp = pltpu.make_async_copy(x_hbm.at[pl.ds(row0 - HALO, HALO), :],
                                       lhs_ref.at[pl.ds(0, HALO), :], sem.at[1])
            cp.start()
            cp.wait()

        @pl.when(pos == 0)
        def _():
            lhs_ref[0:HALO, :] = jnp.zeros((HALO, lhs_ref.shape[1]), lhs_ref.dtype)

        @pl.when(pos < tiles_per_seq - 1)
        def _():
            cp = pltpu.make_async_copy(x_hbm.at[pl.ds(row0 + tm, HALO), :],
                                       lhs_ref.at[pl.ds(HALO + tm, HALO), :], sem.at[2])
            cp.start()
            cp.wait()

        @pl.when(pos == tiles_per_seq - 1)
        def _():
            lhs_ref[HALO + tm:, :] = jnp.zeros((HALO, lhs_ref.shape[1]), lhs_ref.dtype)

        main.wait()

    ext = tm + 2 * HALO
    g = jnp.dot(lhs_ref[...], wg_ref[...].astype(_BF16), preferred_element_type=_F32)
    u = jnp.dot(lhs_ref[HALO:HALO + tm, :], wu_ref[...].astype(_BF16), preferred_element_type=_F32)
    g_prev = pltpu.roll(g, 1, axis=0)[HALO:HALO + tm]
    g_next = pltpu.roll(g, ext - 1, axis=0)[HALO:HALO + tm]
    gc = cb_ref[...] + g_prev * cw_ref[0:1, :] + g[HALO:HALO + tm] * cw_ref[1:2, :] + g_next * cw_ref[2:3, :]
    h_ref[...] = (jax.nn.silu(gc) * u).astype(h_ref.dtype)


def _ffn_up_call(x1b, w_up, conv_w, conv_b, seq):
    m, d = x1b.shape
    d_ff = w_up.shape[1] // 2
    tm = _pick(seq, (2048, 1024, 512, 256))
    tn = _pick(d_ff, (256, 128))
    nj = d_ff // tn
    return pl.pallas_call(
        functools.partial(_ffn_up_kernel, tm=tm, tiles_per_seq=seq // tm),
        grid=(m // tm, nj),
        in_specs=[pl.BlockSpec(memory_space=pl.ANY),
                  pl.BlockSpec((d, tn), lambda i, j: (0, j)),
                  pl.BlockSpec((d, tn), lambda i, j: (0, j + nj)),
                  pl.BlockSpec((CONV_WIDTH, tn), lambda i, j: (0, j)),
                  pl.BlockSpec((1, tn), lambda i, j: (0, j))],
        out_specs=pl.BlockSpec((tm, tn), lambda i, j: (i, j)),
        out_shape=jax.ShapeDtypeStruct((m, d_ff), _BF16),
        scratch_shapes=[pltpu.VMEM((tm + 2 * HALO, d), _BF16), pltpu.SemaphoreType.DMA((3,))],
        compiler_params=_params(("arbitrary", "arbitrary")),
    )(x1b, w_up, w_up, conv_w, conv_b.reshape(1, d_ff))


def _rope_tables(positions):
    inv = 1.0 / (ROPE_THETA ** (jnp.arange(0, HEAD_DIM, 2, dtype=_F32) / HEAD_DIM))
    ang = positions.astype(_F32)[..., None] * inv
    cos, sin = jnp.cos(ang), jnp.sin(ang)
    return jnp.concatenate([cos, cos], axis=-1), jnp.concatenate([-sin, sin], axis=-1)


def kernel(x, mem, positions, w_in, w_mem_kv, w_o, hgrn_gamma_fwd, hgrn_gamma_bwd, hgrn_norm_g,
           ln1_g, ln1_b, w_up, conv_w, conv_b, w_down, ln2_g, ln2_b):
    b, s, d = x.shape
    depth = w_in.shape[0]
    a_width = b_width = 3 * d // 8
    m_width = d // 4
    a_heads = a_width // HEAD_DIM
    b_heads = b_width // HEAD_DIM
    alpha = (2 * depth) ** 0.25
    m = b * s
    cosf, sinf = _rope_tables(positions)

    xf = x.reshape(m, d)
    for layer in range(depth):
        assert layer == 0, "lower-bound running sum is implemented for the first layer only"
        xb = xf.astype(_BF16)
        proj = _matmul_rows_resident(xb, w_in[layer], _F32, _pick(m, (2048, 1024, 512, 256)),
                                     _pick(w_in.shape[2], (512, 256, 128)))
        proj3 = proj.reshape(b, s, -1)
        o_a = _attn_call(proj3, cosf, sinf, a_heads)
        o_b = _hgrn_call(proj3, hgrn_gamma_fwd, hgrn_gamma_bwd, hgrn_norm_g[layer],
                         3 * a_heads, b_heads)
        kvm = _matmul(mem.reshape(-1, d), w_mem_kv[layer], _F32,
                      _pick(b * mem.shape[1], (512, 256, 128)), _pick(2 * m_width, (512, 256, 128)))
        o_m = _memattn_call(proj3, kvm.reshape(b, mem.shape[1], 2 * m_width),
                            (3 * a_width + 5 * b_width) // m_width, m_width)
        y1 = _mix_out_call(o_a.reshape(m, a_width), o_b.reshape(m, b_width), o_m.reshape(m, m_width),
                           w_o[layer], xf, alpha)
        x1, x1b = _layernorm_call(y1, ln1_g[layer], ln1_b[layer], (_F32, _BF16))
        h = _ffn_up_call(x1b, w_up[layer], conv_w[layer], conv_b[layer], s)
        ffn = _matmul_ktiled(h, w_down[layer], _pick(m, (2048, 1024, 512, 256)),
                             _pick(d, (1024, 512, 256)), 1024)
        (xf,) = _layernorm_call(ffn, ln2_g[layer], ln2_b[layer], (_F32,), resid=x1, alpha=alpha)
    return xf.reshape(b, s, d)
```

```python
import functools

import jax
import jax.numpy as jnp
from jax import lax
from jax.experimental import pallas as pl
from jax.experimental.pallas import tpu as pltpu

HEAD_DIM = 128
DILATED_PATTERNS = ((128, 1), (512, 4), (2048, 16))
HGRN_CHUNK = 64
HGRN_SUB = 16
HGRN_EXP_CLAMP = 60.0
M_HEADS = 4
CONV_WIDTH = 3
ROPE_THETA = 10000.0
LN_EPS = 1e-5
RMS_EPS = 1e-6
NEG_INF = -1e30
LOG2_E = 1.4426950408889634
HALO = 16
VMEM_LIMIT = 56 * 1024 * 1024

_BF16 = jnp.bfloat16
_F32 = jnp.float32


def _pick(n, prefs):
    for p in prefs:
        if n % p == 0:
            return p
    return n


def _params(sem):
    return pltpu.CompilerParams(dimension_semantics=sem, vmem_limit_bytes=VMEM_LIMIT)


def _mm_kernel(x_ref, w_ref, o_ref):
    o_ref[...] = jnp.dot(x_ref[...].astype(_BF16), w_ref[...].astype(_BF16),
                         preferred_element_type=_F32).astype(o_ref.dtype)


def _mm_acc_kernel(x_ref, w_ref, o_ref, acc_ref, *, nk, tail):
    k = pl.program_id(2)
    tk = x_ref.shape[1]

    def part(valid):
        x, w = x_ref[...], w_ref[...]
        if valid < tk:
            col = lax.broadcasted_iota(jnp.int32, x.shape, 1)
            x = jnp.where(col < valid, x.astype(_F32), 0.0)
            row = lax.broadcasted_iota(jnp.int32, w.shape, 0)
            w = jnp.where(row < valid, w, 0.0)
        return jnp.dot(x.astype(_BF16), w.astype(_BF16), preferred_element_type=_F32)

    @pl.when(k == 0)
    def _():
        acc_ref[...] = part(tk)

    @pl.when(jnp.logical_and(k > 0, k < nk - 1))
    def _():
        acc_ref[...] += part(tk)

    @pl.when(k == nk - 1)
    def _():
        o_ref[...] = acc_ref[...] + part(tail)


def _matmul(x, w, out_dtype, tm, tn):
    m, kdim = x.shape
    n = w.shape[1]
    return pl.pallas_call(
        _mm_kernel,
        grid=(m // tm, n // tn),
        in_specs=[pl.BlockSpec((tm, kdim), lambda i, j: (i, 0)),
                  pl.BlockSpec((kdim, tn), lambda i, j: (0, j))],
        out_specs=pl.BlockSpec((tm, tn), lambda i, j: (i, j)),
        out_shape=jax.ShapeDtypeStruct((m, n), out_dtype),
        compiler_params=_params(("parallel", "parallel")),
    )(x, w)


def _mm_rows_kernel(x_hbm, w_ref, o_ref, x_scr, sem, *, tm):
    i = pl.program_id(0)

    @pl.when(pl.program_id(1) == 0)
    def _():
        cp = pltpu.make_async_copy(x_hbm.at[pl.ds(pl.multiple_of(i * tm, tm), tm), :], x_scr, sem.at[0])
        cp.start()
        cp.wait()

    o_ref[...] = jnp.dot(x_scr[...], w_ref[...].astype(_BF16),
                         preferred_element_type=_F32).astype(o_ref.dtype)


def _matmul_rows_resident(x, w, out_dtype, tm, tn):
    m, kdim = x.shape
    n = w.shape[1]
    return pl.pallas_call(
        functools.partial(_mm_rows_kernel, tm=tm),
        grid=(m // tm, n // tn),
        in_specs=[pl.BlockSpec(memory_space=pl.ANY),
                  pl.BlockSpec((kdim, tn), lambda i, j: (0, j))],
        out_specs=pl.BlockSpec((tm, tn), lambda i, j: (i, j)),
        out_shape=jax.ShapeDtypeStruct((m, n), out_dtype),
        scratch_shapes=[pltpu.VMEM((tm, kdim), x.dtype), pltpu.SemaphoreType.DMA((1,))],
        compiler_params=_params(("arbitrary", "arbitrary")),
    )(x, w)


def _matmul_ktiled(x, w, tm, tn, tk):
    m, kdim = x.shape
    n = w.shape[1]
    nk = pl.cdiv(kdim, tk)
    assert nk >= 2
    return pl.pallas_call(
        functools.partial(_mm_acc_kernel, nk=nk, tail=kdim - (nk - 1) * tk),
        grid=(m // tm, n // tn, nk),
        in_specs=[pl.BlockSpec((tm, tk), lambda i, j, k: (i, k)),
                  pl.BlockSpec((tk, tn), lambda i, j, k: (k, j))],
        out_specs=pl.BlockSpec((tm, tn), lambda i, j, k: (i, j)),
        out_shape=jax.ShapeDtypeStruct((m, n), _F32),
        scratch_shapes=[pltpu.VMEM((tm, tn), _F32)],
        compiler_params=_params(("parallel", "parallel", "arbitrary")),
    )(x, w)


def _attn_kernel(q_ref, k_ref, v_ref, cos_ref, sin_ref, o_ref, qn, kn, qm, km, vm, qs, ks, vs,
                 run0, run1, bias, *, seq, bq, group):
    scale = HEAD_DIM ** -0.5 * LOG2_E
    (w0, d0), (w1, d1), (w2, d2) = sorted(DILATED_PATTERNS, key=lambda wd: wd[1])
    assert d0 == 1 and d1 % d0 == 0 and d2 % d1 == 0
    r1, r2 = d1 // d0, d2 // d1
    len1, len2 = seq // d1, seq // d2
    rows = 256
    rows2 = min(rows, len2)

    def rope_rows(t, carry):
        sl = pl.ds(pl.multiple_of(t * rows, rows), rows)
        cos, sin = cos_ref[0, sl, :], sin_ref[0, sl, :]
        qv, kv = q_ref[0, sl, :], k_ref[0, sl, :]
        qn[sl, :] = (qv * cos + pltpu.roll(qv, HEAD_DIM // 2, axis=1) * sin) * scale
        kn[sl, :] = kv * cos + pltpu.roll(kv, HEAD_DIM // 2, axis=1) * sin
        return carry

    lax.fori_loop(0, seq // rows, rope_rows, 0)

    def build_mid(t, carry):
        per = len1 // rows
        r, c = t // per, t % per
        src = pl.ds(r + r1 * c * rows, rows, stride=r1)
        dst = pl.ds(pl.multiple_of(r * len1 + c * rows, rows), rows)
        qm[dst, :] = qn[src, :]
        km[dst, :] = kn[src, :]
        vm[dst, :] = v_ref[0, src, :]
        return carry

    lax.fori_loop(0, d1 * (len1 // rows), build_mid, 0)

    def build_top(t, carry):
        per = len2 // rows2
        r, c = t // per, t % per
        src = pl.ds((r % d1) * len1 + r // d1 + r2 * c * rows2, rows2, stride=r2)
        dst = pl.ds(pl.multiple_of(r * len2 + c * rows2, rows2), rows2)
        qs[dst, :] = qm[src, :].astype(_BF16)
        ks[dst, :] = km[src, :].astype(_BF16)
        vs[dst, 0:HEAD_DIM] = vm[src, :].astype(_BF16)
        vs[dst, HEAD_DIM:] = jnp.ones((rows2, HEAD_DIM), _BF16)
        return carry

    lax.fori_loop(0, d2 * (len2 // rows2), build_top, 0)

    def cast_rows(q_src, k_src, v_src):
        def body(t, carry):
            sl = pl.ds(pl.multiple_of(t * rows, rows), rows)
            qs[sl, :] = q_src(sl).astype(_BF16)
            ks[sl, :] = k_src(sl).astype(_BF16)
            vs[sl, 0:HEAD_DIM] = v_src(sl).astype(_BF16)
            return carry
        lax.fori_loop(0, seq // rows, body, 0)

    def branch(window, dil, parent, prev, store):
        sub_len = seq // dil
        half = window // (2 * dil)
        win = min(bq + 2 * half, sub_len)
        per = sub_len // bq
        offsets = {q0 - min(max(q0 - half, 0), sub_len - win) for q0 in range(0, sub_len, bq)}
        assert offsets <= {0, half, 2 * half} and half % 16 == 0 and sub_len % bq == 0

        for v in range(3):
            rel = (lax.broadcasted_iota(jnp.int32, (bq, win), 0)
                   - lax.broadcasted_iota(jnp.int32, (bq, win), 1) + v * half)
            bias[v, :, 0:win] = jnp.where(jnp.abs(rel) <= half, 0.0, NEG_INF)

        def block(t):
            r = t // per
            q0 = (t % per) * bq
            ws = jnp.clip(q0 - half, 0, sub_len - win)
            base = r * sub_len
            own = pl.ds(pl.multiple_of(base + q0, bq), bq)
            keys = pl.ds(pl.multiple_of(base + ws, half), win)
            s = lax.dot_general(qs[own, :], ks[keys, :], (((1,), (1,)), ((), ())),
                                preferred_element_type=_F32) + bias[(q0 - ws) // half, :, 0:win]
            m = jnp.broadcast_to(jnp.max(s, axis=-1, keepdims=True), (bq, HEAD_DIM))
            if prev is not None:
                m_p = prev[1, own, :]
                m = jnp.maximum(m, m_p)
            e = jnp.concatenate([jnp.exp2(s[:, c:c + HEAD_DIM] - m) for c in range(0, win, HEAD_DIM)],
                                axis=1)
            pv = jnp.dot(e.astype(_BF16), vs[keys, :], preferred_element_type=_F32)
            acc, l = pv[:, 0:HEAD_DIM], pv[:, HEAD_DIM:]
            if prev is not None:
                a = jnp.exp2(m_p - m)
                acc = a * prev[0, own, :] + acc
                l = a * prev[2, own, :] + l
            if store is None:
                o_ref[0, own, :] = (acc / l).astype(o_ref.dtype)
            else:
                pd, plen = parent
                ratio = dil // pd
                dst = pl.ds((r % pd) * plen + r // pd + ratio * q0, bq, stride=ratio)
                store[0, dst, :] = acc
                store[1, dst, :] = m
                store[2, dst, :] = l

        def body(tt, carry):
            for g in range(group):
                block(tt * group + g)
            return carry

        lax.fori_loop(0, seq // (bq * group), body, 0)

    branch(w2, d2, (d1, len1), None, run1)
    cast_rows(lambda sl: qm[sl, :], lambda sl: km[sl, :], lambda sl: vm[sl, :])
    branch(w1, d1, (d0, seq), run1, run0)
    cast_rows(lambda sl: qn[sl, :], lambda sl: kn[sl, :], lambda sl: v_ref[0, sl, :])
    branch(w0, d0, None, run0, None)


def _attn_call(proj3, cosf, sinf, a_heads):
    b, s, _ = proj3.shape
    bq = 128
    group = min(32, s // bq)
    assert (s // bq) % group == 0
    half_max = max(w // (2 * d) for w, d in DILATED_PATTERNS)
    col = lambda off: pl.BlockSpec((1, s, HEAD_DIM), lambda bi, h, off=off: (bi, 0, off + h))
    rope_spec = pl.BlockSpec((1, s, HEAD_DIM), lambda bi, h: (bi, 0, 0), pipeline_mode=pl.Buffered(1))
    seq_f32 = pltpu.VMEM((s, HEAD_DIM), _F32)
    seq_bf16 = pltpu.VMEM((s, HEAD_DIM), _BF16)
    return pl.pallas_call(
        functools.partial(_attn_kernel, seq=s, bq=bq, group=group),
        grid=(b, a_heads),
        in_specs=[col(0), col(a_heads), col(2 * a_heads), rope_spec, rope_spec],
        out_specs=pl.BlockSpec((1, s, HEAD_DIM), lambda bi, h: (bi, 0, h)),
        out_shape=jax.ShapeDtypeStruct((b, s, a_heads * HEAD_DIM), _BF16),
        scratch_shapes=[seq_f32] * 5 + [seq_bf16] * 2 + [pltpu.VMEM((s, 2 * HEAD_DIM), _BF16)]
        + [pltpu.VMEM((3, s, HEAD_DIM), _F32)] * 2
        + [pltpu.VMEM((3, bq, bq + 2 * half_max), _F32)],
        compiler_params=_params(("parallel", "parallel")),
    )(proj3, proj3, proj3, cosf, sinf)


def _scan_rows(x, reverse):
    n = x.shape[0]
    row = lax.broadcasted_iota(jnp.int32, x.shape, 0)
    k = 1
    while k < n:
        if reverse:
            x = x + jnp.where(row < n - k, pltpu.roll(x, n - k, axis=0), 0.0)
        else:
            x = x + jnp.where(row >= k, pltpu.roll(x, k, axis=0), 0.0)
        k *= 2
    return x


def _hgrn_chunk(qc, vc16, fp, lb, state_t, reverse):
    c = HGRN_CHUNK
    f = lb + (1.0 - lb) * jax.nn.sigmoid(fp)
    kk = 1.0 - f
    bc = _scan_rows(jnp.log(f), reverse)
    inter = lax.dot_general((qc * jnp.exp(bc)).astype(_BF16), state_t.astype(_BF16),
                            (((1,), (1,)), ((), ())), preferred_element_type=_F32)
    blocks = []
    for i in range(c // HGRN_SUB):
        lo, hi = i * HGRN_SUB, (i + 1) * HGRN_SUB
        beta = bc[hi - 1:hi] if reverse else bc[lo:lo + 1]
        qt = qc[lo:hi] * jnp.exp(bc[lo:hi] - beta)
        kt = kk * jnp.exp(jnp.minimum(beta - bc, HGRN_EXP_CLAMP))
        blocks.append(lax.dot_general(qt.astype(_BF16), kt.astype(_BF16),
                                      (((1,), (1,)), ((), ())), preferred_element_type=_F32))
    scores = jnp.concatenate(blocks, axis=0)
    t_idx = lax.broadcasted_iota(jnp.int32, (c, c), 0)
    s_idx = lax.broadcasted_iota(jnp.int32, (c, c), 1)
    keep = (s_idx >= t_idx) if reverse else (s_idx <= t_idx)
    scores = jnp.where(keep, scores, 0.0)
    intra = jnp.dot(scores.astype(_BF16), vc16, preferred_element_type=_F32)
    last = bc[0:1] if reverse else bc[c - 1:c]
    kd = (kk * jnp.exp(last - bc)).astype(_BF16)
    new_state = jnp.exp(last) * state_t + lax.dot_general(
        vc16, kd, (((0,), (0,)), ((), ())), preferred_element_type=_F32)
    return inter + intra, new_state


def _hgrn_kernel(q_ref, i_ref, g_ref, ff_ref, fb_ref, gf_ref, gb_ref, ng_ref, o_ref, yf, yb, *, seq):
    c = HGRN_CHUNK
    n = seq // c

    def lower_bound(gam_ref):
        gam = gam_ref[...]
        e = jnp.exp(gam - jnp.max(gam, axis=0, keepdims=True))
        return e[0:1] / jnp.sum(e, axis=0, keepdims=True)

    lb_f = lower_bound(gf_ref)
    lb_b = lower_bound(gb_ref)

    def step(t, carry):
        st_f, st_b = carry
        sl_f = pl.ds(pl.multiple_of(t * c, c), c)
        sl_b = pl.ds(pl.multiple_of((n - 1 - t) * c, c), c)
        y, st_f = _hgrn_chunk(jax.nn.silu(q_ref[0, sl_f, :]), i_ref[0, sl_f, :].astype(_BF16),
                              ff_ref[0, sl_f, :], lb_f, st_f, False)
        yf[sl_f, :] = y
        y, st_b = _hgrn_chunk(jax.nn.silu(q_ref[0, sl_b, :]), i_ref[0, sl_b, :].astype(_BF16),
                              fb_ref[0, sl_b, :], lb_b, st_b, True)
        yb[sl_b, :] = y
        return st_f, st_b

    zero = jnp.zeros((HEAD_DIM, HEAD_DIM), _F32)
    lax.fori_loop(0, n, step, (zero, zero), unroll=8)

    rows = 256

    def finish(t, carry):
        sl = pl.ds(pl.multiple_of(t * rows, rows), rows)
        y = yf[sl, :] + yb[sl, :]
        y = y * lax.rsqrt(jnp.mean(jnp.square(y), axis=-1, keepdims=True) + RMS_EPS)
        o_ref[0, sl, :] = (y * ng_ref[...] * jax.nn.silu(g_ref[0, sl, :])).astype(o_ref.dtype)
        return carry

    lax.fori_loop(0, seq // rows, finish, 0)


def _hgrn_call(proj3, gamma_f, gamma_b, norm_g, col0, b_heads):
    b, s, _ = proj3.shape
    col = lambda off: pl.BlockSpec((1, s, HEAD_DIM), lambda bi, h, off=off: (bi, 0, col0 + off + h))
    gam_spec = pl.BlockSpec((gamma_f.shape[0], HEAD_DIM), lambda bi, h: (0, h))
    return pl.pallas_call(
        functools.partial(_hgrn_kernel, seq=s),
        grid=(b, b_heads),
        in_specs=[col(0), col(b_heads), col(2 * b_heads), col(3 * b_heads), col(4 * b_heads),
                  gam_spec, gam_spec, pl.BlockSpec((1, HEAD_DIM), lambda bi, h: (0, h))],
        out_specs=pl.BlockSpec((1, s, HEAD_DIM), lambda bi, h: (bi, 0, h)),
        out_shape=jax.ShapeDtypeStruct((b, s, b_heads * HEAD_DIM), _BF16),
        scratch_shapes=[pltpu.VMEM((s, HEAD_DIM), _F32)] * 2,
        compiler_params=_params(("parallel", "parallel")),
    )(proj3, proj3, proj3, proj3, proj3, gamma_f, gamma_b, norm_g.reshape(1, -1))


def _memattn_kernel(q_ref, kv_ref, o_ref, *, m_width):
    hd = m_width // M_HEADS
    scale = hd ** -0.5
    for h in range(M_HEADS):
        q = (q_ref[0, :, h * hd:(h + 1) * hd] * scale).astype(_BF16)
        k = kv_ref[0, :, h * hd:(h + 1) * hd].astype(_BF16)
        v = kv_ref[0, :, m_width + h * hd:m_width + (h + 1) * hd].astype(_BF16)
        s = lax.dot_general(q, k, (((1,), (1,)), ((), ())), preferred_element_type=_F32)
        e = jnp.exp(s - jnp.max(s, axis=-1, keepdims=True))
        o = jnp.dot(e.astype(_BF16), v, preferred_element_type=_F32) / jnp.sum(e, axis=-1, keepdims=True)
        o_ref[0, :, h * hd:(h + 1) * hd] = o.astype(o_ref.dtype)


def _memattn_call(proj3, kvm, q_col_block, m_width):
    b, s, _ = proj3.shape
    mlen = kvm.shape[1]
    tq = _pick(s, (512, 256, 128))
    return pl.pallas_call(
        functools.partial(_memattn_kernel, m_width=m_width),
        grid=(b, s // tq),
        in_specs=[pl.BlockSpec((1, tq, m_width), lambda bi, i: (bi, i, q_col_block)),
                  pl.BlockSpec((1, mlen, 2 * m_width), lambda bi, i: (bi, 0, 0))],
        out_specs=pl.BlockSpec((1, tq, m_width), lambda bi, i: (bi, i, 0)),
        out_shape=jax.ShapeDtypeStruct((b, s, m_width), _BF16),
        compiler_params=_params(("parallel", "parallel")),
    )(proj3, kvm)


def _mix_out_kernel(a_ref, b_ref, m_ref, wa_ref, wb_ref, wm_ref, r_ref, o_ref, *, alpha):
    acc = jnp.dot(a_ref[...], wa_ref[...].astype(_BF16), preferred_element_type=_F32)
    acc += jnp.dot(b_ref[...], wb_ref[...].astype(_BF16), preferred_element_type=_F32)
    acc += jnp.dot(m_ref[...], wm_ref[...].astype(_BF16), preferred_element_type=_F32)
    o_ref[...] = alpha * r_ref[...] + acc


def _mix_out_call(oa, ob, om, w_o, resid, alpha):
    m, aw = oa.shape
    bw, mw = ob.shape[1], om.shape[1]
    n = w_o.shape[1]
    tm = _pick(m, (1024, 512, 256))
    tn = _pick(n, (512, 256, 128))
    assert aw == bw and (aw + bw) % mw == 0
    return pl.pallas_call(
        functools.partial(_mix_out_kernel, alpha=alpha),
        grid=(m // tm, n // tn),
        in_specs=[pl.BlockSpec((tm, aw), lambda i, j: (i, 0)),
                  pl.BlockSpec((tm, bw), lambda i, j: (i, 0)),
                  pl.BlockSpec((tm, mw), lambda i, j: (i, 0)),
                  pl.BlockSpec((aw, tn), lambda i, j: (0, j)),
                  pl.BlockSpec((bw, tn), lambda i, j: (1, j)),
                  pl.BlockSpec((mw, tn), lambda i, j: ((aw + bw) // mw, j)),
                  pl.BlockSpec((tm, tn), lambda i, j: (i, j))],
        out_specs=pl.BlockSpec((tm, tn), lambda i, j: (i, j)),
        out_shape=jax.ShapeDtypeStruct((m, n), _F32),
        compiler_params=_params(("parallel", "parallel")),
    )(oa, ob, om, w_o, w_o, w_o, resid)


def _ln_rows(y, g_ref, b_ref):
    mu = jnp.mean(y, axis=-1, keepdims=True)
    d = y - mu
    var = jnp.mean(d * d, axis=-1, keepdims=True)
    return d * lax.rsqrt(var + LN_EPS) * g_ref[...] + b_ref[...]


def _ln_kernel(*refs, alpha):
    if alpha is None:
        y_ref, g_ref, b_ref, *o_refs = refs
        y = y_ref[...]
    else:
        y_ref, pre_ref, rg_ref, rb_ref, g_ref, b_ref, *o_refs = refs
        y = alpha * _ln_rows(pre_ref[...], rg_ref, rb_ref) + y_ref[...]
    out = _ln_rows(y, g_ref, b_ref)
    for o_ref in o_refs:
        o_ref[...] = out.astype(o_ref.dtype)


def _layernorm_call(y, g, b, out_dtypes, resid_pre=None, resid_g=None, resid_b=None, alpha=None):
    m, n = y.shape
    tr = _pick(m, (256, 128, 64))
    row = pl.BlockSpec((tr, n), lambda i: (i, 0))
    vec = pl.BlockSpec((1, n), lambda i: (0, 0))
    if resid_pre is None:
        args, specs = (y,), [row]
    else:
        args, specs = (y, resid_pre, resid_g.reshape(1, n), resid_b.reshape(1, n)), [row, row, vec, vec]
    return pl.pallas_call(
        functools.partial(_ln_kernel, alpha=None if resid_pre is None else alpha),
        grid=(m // tr,),
        in_specs=specs + [vec, vec],
        out_specs=[row] * len(out_dtypes),
        out_shape=[jax.ShapeDtypeStruct((m, n), dt) for dt in out_dtypes],
        compiler_params=_params(("parallel",)),
    )(*args, g.reshape(1, n), b.reshape(1, n))


def _ffn_up_kernel(x_hbm, wg_ref, wu_ref, cw_ref, cb_ref, h_ref, lhs_ref, sem, *, tm, tiles_per_seq):
    i = pl.program_id(0)
    j = pl.program_id(1)

    @pl.when(j == 0)
    def _():
        pos = i % tiles_per_seq
        row0 = pl.multiple_of(i * tm, tm)
        main = pltpu.make_async_copy(x_hbm.at[pl.ds(row0, tm), :], lhs_ref.at[pl.ds(HALO, tm), :],
                                     sem.at[0])
        main.start()

        @pl.when(pos > 0)
        def _():
            cp = pltpu.make_async_copy(x_hbm.at[pl.ds(row0 - HALO, HALO), :],
                                       lhs_ref.at[pl.ds(0, HALO), :], sem.at[1])
            cp.start()
            cp.wait()

        @pl.when(pos == 0)
        def _():
            lhs_ref[0:HALO, :] = jnp.zeros((HALO, lhs_ref.shape[1]), lhs_ref.dtype)

        @pl.when(pos < tiles_per_seq - 1)
        def _():
            cp = pltpu.make_async_copy(x_hbm.at[pl.ds(row0 + tm, HALO), :],
                                       lhs_ref.at[pl.ds(HALO + tm, HALO), :], sem.at[2])
            cp.start()
            cp.wait()

        @pl.when(pos == tiles_per_seq - 1)
        def _():
            lhs_ref[HALO + tm:, :] = jnp.zeros((HALO, lhs_ref.shape[1]), lhs_ref.dtype)

        main.wait()

    ext = tm + 2 * HALO
    g = jnp.dot(lhs_ref[...], wg_ref[...].astype(_BF16), preferred_element_type=_F32)
    u = jnp.dot(lhs_ref[HALO:HALO + tm, :], wu_ref[...].astype(_BF16), preferred_element_type=_F32)
    g_prev = pltpu.roll(g, 1, axis=0)[HALO:HALO + tm]
    g_next = pltpu.roll(g, ext - 1, axis=0)[HALO:HALO + tm]
    gc = cb_ref[...] + g_prev * cw_ref[0:1, :] + g[HALO:HALO + tm] * cw_ref[1:2, :] + g_next * cw_ref[2:3, :]
    h_ref[...] = (jax.nn.silu(gc) * u).astype(h_ref.dtype)


def _ffn_up_call(x1b, w_up, conv_w, conv_b, seq):
    m, d = x1b.shape
    d_ff = w_up.shape[1] // 2
    tm = _pick(seq, (2048, 1024, 512, 256))
    tn = _pick(d_ff, (256, 128))
    nj = d_ff // tn
    return pl.pallas_call(
        functools.partial(_ffn_up_kernel, tm=tm, tiles_per_seq=seq // tm),
        grid=(m // tm, nj),
        in_specs=[pl.BlockSpec(memory_space=pl.ANY),
                  pl.BlockSpec((d, tn), lambda i, j: (0, j)),
                  pl.BlockSpec((d, tn), lambda i, j: (0, j + nj)),
                  pl.BlockSpec((CONV_WIDTH, tn), lambda i, j: (0, j)),
                  pl.BlockSpec((1, tn), lambda i, j: (0, j))],
        out_specs=pl.BlockSpec((tm, tn), lambda i, j: (i, j)),
        out_shape=jax.ShapeDtypeStruct((m, d_ff), _BF16),
        scratch_shapes=[pltpu.VMEM((tm + 2 * HALO, d), _BF16), pltpu.SemaphoreType.DMA((3,))],
        compiler_params=_params(("arbitrary", "arbitrary")),
    )(x1b, w_up, w_up, conv_w, conv_b.reshape(1, d_ff))


def _rope_tables(positions):
    inv = 1.0 / (ROPE_THETA ** (jnp.arange(0, HEAD_DIM, 2, dtype=_F32) / HEAD_DIM))
    ang = positions.astype(_F32)[..., None] * inv
    cos, sin = jnp.cos(ang), jnp.sin(ang)
    return jnp.concatenate([cos, cos], axis=-1), jnp.concatenate([-sin, sin], axis=-1)


def kernel(x, mem, positions, w_in, w_mem_kv, w_o, hgrn_gamma_fwd, hgrn_gamma_bwd, hgrn_norm_g,
           ln1_g, ln1_b, w_up, conv_w, conv_b, w_down, ln2_g, ln2_b):
    b, s, d = x.shape
    depth = w_in.shape[0]
    a_width = b_width = 3 * d // 8
    m_width = d // 4
    a_heads = a_width // HEAD_DIM
    b_heads = b_width // HEAD_DIM
    alpha = (2 * depth) ** 0.25
    m = b * s
    cosf, sinf = _rope_tables(positions)

    xf = x.reshape(m, d)
    for layer in range(depth):
        assert layer == 0, "lower-bound running sum is implemented for the first layer only"
        xb = xf.astype(_BF16)
        proj = _matmul_rows_resident(xb, w_in[layer], _F32, _pick(m, (2048, 1024, 512, 256)),
                                     _pick(w_in.shape[2], (512, 256, 128)))
        proj3 = proj.reshape(b, s, -1)
        o_a = _attn_call(proj3, cosf, sinf, a_heads)
        o_b = _hgrn_call(proj3, hgrn_gamma_fwd, hgrn_gamma_bwd, hgrn_norm_g[layer],
                         3 * a_heads, b_heads)
        kvm = _matmul(mem.reshape(-1, d), w_mem_kv[layer], _F32,
                      _pick(b * mem.shape[1], (512, 256, 128)), _pick(2 * m_width, (512, 256, 128)))
        o_m = _memattn_call(proj3, kvm.reshape(b, mem.shape[1], 2 * m_width),
                            (3 * a_width + 5 * b_width) // m_width, m_width)
        y1 = _mix_out_call(o_a.reshape(m, a_width), o_b.reshape(m, b_width), o_m.reshape(m, m_width),
                           w_o[layer], xf, alpha)
        (x1b,) = _layernorm_call(y1, ln1_g[layer], ln1_b[layer], (_BF16,))
        h = _ffn_up_call(x1b, w_up[layer], conv_w[layer], conv_b[layer], s)
        ffn = _matmul_ktiled(h, w_down[layer], _pick(m, (2048, 1024, 512, 256)),
                             _pick(d, (1024, 512, 256)), 1024)
        (xf,) = _layernorm_call(ffn, ln2_g[layer], ln2_b[layer], (_F32,), resid_pre=y1,
                                resid_g=ln1_g[layer], resid_b=ln1_b[layer], alpha=alpha)
    return xf.reshape(b, s, d)
```

```python
import functools

import jax
import jax.numpy as jnp
from jax import lax
from jax.experimental import pallas as pl
from jax.experimental.pallas import tpu as pltpu

HEAD_DIM = 128
DILATED_PATTERNS = ((128, 1), (512, 4), (2048, 16))
HGRN_CHUNK = 64
HGRN_SUB = 16
HGRN_EXP2_CLAMP = 86.0
M_HEADS = 4
CONV_WIDTH = 3
ROPE_THETA = 10000.0
LN_EPS = 1e-5
RMS_EPS = 1e-6
NEG_INF = -1e30
LOG2_E = 1.4426950408889634
HALO = 16
VMEM_LIMIT = 56 * 1024 * 1024

_BF16 = jnp.bfloat16
_F32 = jnp.float32


def _pick(n, prefs):
    for p in prefs:
        if n % p == 0:
            return p
    return n


def _params(sem):
    return pltpu.CompilerParams(dimension_semantics=sem, vmem_limit_bytes=VMEM_LIMIT)


def _mm_kernel(x_ref, w_ref, o_ref):
    o_ref[...] = jnp.dot(x_ref[...].astype(_BF16), w_ref[...].astype(_BF16),
                         preferred_element_type=_F32).astype(o_ref.dtype)


def _mm_acc_kernel(x_ref, w_ref, o_ref, acc_ref, *, nk, tail):
    k = pl.program_id(2)
    tk = x_ref.shape[1]

    def part(valid):
        x, w = x_ref[...], w_ref[...]
        if valid < tk:
            col = lax.broadcasted_iota(jnp.int32, x.shape, 1)
            x = jnp.where(col < valid, x.astype(_F32), 0.0)
            row = lax.broadcasted_iota(jnp.int32, w.shape, 0)
            w = jnp.where(row < valid, w, 0.0)
        return jnp.dot(x.astype(_BF16), w.astype(_BF16), preferred_element_type=_F32)

    @pl.when(k == 0)
    def _():
        acc_ref[...] = part(tk)

    @pl.when(jnp.logical_and(k > 0, k < nk - 1))
    def _():
        acc_ref[...] += part(tk)

    @pl.when(k == nk - 1)
    def _():
        o_ref[...] = acc_ref[...] + part(tail)


def _matmul(x, w, out_dtype, tm, tn):
    m, kdim = x.shape
    n = w.shape[1]
    return pl.pallas_call(
        _mm_kernel,
        grid=(m // tm, n // tn),
        in_specs=[pl.BlockSpec((tm, kdim), lambda i, j: (i, 0)),
                  pl.BlockSpec((kdim, tn), lambda i, j: (0, j))],
        out_specs=pl.BlockSpec((tm, tn), lambda i, j: (i, j)),
        out_shape=jax.ShapeDtypeStruct((m, n), out_dtype),
        compiler_params=_params(("parallel", "parallel")),
    )(x, w)


def _mm_rows_kernel(x_hbm, w_ref, o_ref, x_scr, sem, *, tm):
    i = pl.program_id(0)

    @pl.when(pl.program_id(1) == 0)
    def _():
        cp = pltpu.make_async_copy(x_hbm.at[pl.ds(pl.multiple_of(i * tm, tm), tm), :], x_scr, sem.at[0])
        cp.start()
        cp.wait()

    o_ref[...] = jnp.dot(x_scr[...], w_ref[...].astype(_BF16),
                         preferred_element_type=_F32).astype(o_ref.dtype)


def _matmul_rows_resident(x, w, out_dtype, tm, tn):
    m, kdim = x.shape
    n = w.shape[1]
    return pl.pallas_call(
        functools.partial(_mm_rows_kernel, tm=tm),
        grid=(m // tm, n // tn),
        in_specs=[pl.BlockSpec(memory_space=pl.ANY),
                  pl.BlockSpec((kdim, tn), lambda i, j: (0, j))],
        out_specs=pl.BlockSpec((tm, tn), lambda i, j: (i, j)),
        out_shape=jax.ShapeDtypeStruct((m, n), out_dtype),
        scratch_shapes=[pltpu.VMEM((tm, kdim), x.dtype), pltpu.SemaphoreType.DMA((1,))],
        compiler_params=_params(("arbitrary", "arbitrary")),
    )(x, w)


def _matmul_ktiled(x, w, tm, tn, tk):
    m, kdim = x.shape
    n = w.shape[1]
    nk = pl.cdiv(kdim, tk)
    assert nk >= 2
    return pl.pallas_call(
        functools.partial(_mm_acc_kernel, nk=nk, tail=kdim - (nk - 1) * tk),
        grid=(m // tm, n // tn, nk),
        in_specs=[pl.BlockSpec((tm, tk), lambda i, j, k: (i, k)),
                  pl.BlockSpec((tk, tn), lambda i, j, k: (k, j))],
        out_specs=pl.BlockSpec((tm, tn), lambda i, j, k: (i, j)),
        out_shape=jax.ShapeDtypeStruct((m, n), _F32),
        scratch_shapes=[pltpu.VMEM((tm, tn), _F32)],
        compiler_params=_params(("parallel", "parallel", "arbitrary")),
    )(x, w)


def _attn_kernel(q_ref, k_ref, v_ref, cos_ref, sin_ref, o_ref, qn, kn, qm, km, vm, qs, ks, vs,
                 run0, run1, bias, *, seq, bq, group):
    scale = HEAD_DIM ** -0.5 * LOG2_E
    (w0, d0), (w1, d1), (w2, d2) = sorted(DILATED_PATTERNS, key=lambda wd: wd[1])
    assert d0 == 1 and d1 % d0 == 0 and d2 % d1 == 0
    r1, r2 = d1 // d0, d2 // d1
    len1, len2 = seq // d1, seq // d2
    rows = 256
    rows2 = min(rows, len2)

    def rope_rows(t, carry):
        sl = pl.ds(pl.multiple_of(t * rows, rows), rows)
        cos, sin = cos_ref[0, sl, :], sin_ref[0, sl, :]
        qv, kv = q_ref[0, sl, :], k_ref[0, sl, :]
        qn[sl, :] = (qv * cos + pltpu.roll(qv, HEAD_DIM // 2, axis=1) * sin) * scale
        kn[sl, :] = kv * cos + pltpu.roll(kv, HEAD_DIM // 2, axis=1) * sin
        return carry

    lax.fori_loop(0, seq // rows, rope_rows, 0)

    def build_mid(t, carry):
        per = len1 // rows
        r, c = t // per, t % per
        src = pl.ds(r + r1 * c * rows, rows, stride=r1)
        dst = pl.ds(pl.multiple_of(r * len1 + c * rows, rows), rows)
        qm[dst, :] = qn[src, :]
        km[dst, :] = kn[src, :]
        vm[dst, :] = v_ref[0, src, :]
        return carry

    lax.fori_loop(0, d1 * (len1 // rows), build_mid, 0)

    def build_top(t, carry):
        per = len2 // rows2
        r, c = t // per, t % per
        src = pl.ds((r % d1) * len1 + r // d1 + r2 * c * rows2, rows2, stride=r2)
        dst = pl.ds(pl.multiple_of(r * len2 + c * rows2, rows2), rows2)
        qs[dst, :] = qm[src, :].astype(_BF16)
        ks[dst, :] = km[src, :].astype(_BF16)
        vs[dst, 0:HEAD_DIM] = vm[src, :].astype(_BF16)
        vs[dst, HEAD_DIM:] = jnp.ones((rows2, HEAD_DIM), _BF16)
        return carry

    lax.fori_loop(0, d2 * (len2 // rows2), build_top, 0)

    def cast_rows(q_src, k_src, v_src):
        def body(t, carry):
            sl = pl.ds(pl.multiple_of(t * rows, rows), rows)
            qs[sl, :] = q_src(sl).astype(_BF16)
            ks[sl, :] = k_src(sl).astype(_BF16)
            vs[sl, 0:HEAD_DIM] = v_src(sl).astype(_BF16)
            return carry
        lax.fori_loop(0, seq // rows, body, 0)

    def branch(window, dil, parent, prev, store):
        sub_len = seq // dil
        half = window // (2 * dil)
        win = min(bq + 2 * half, sub_len)
        per = sub_len // bq
        offsets = {q0 - min(max(q0 - half, 0), sub_len - win) for q0 in range(0, sub_len, bq)}
        assert offsets <= {0, half, 2 * half} and half % 16 == 0 and sub_len % bq == 0

        for v in range(3):
            rel = (lax.broadcasted_iota(jnp.int32, (bq, win), 0)
                   - lax.broadcasted_iota(jnp.int32, (bq, win), 1) + v * half)
            bias[v, :, 0:win] = jnp.where(jnp.abs(rel) <= half, 0.0, NEG_INF)

        def block(t):
            r = t // per
            q0 = (t % per) * bq
            ws = jnp.clip(q0 - half, 0, sub_len - win)
            base = r * sub_len
            own = pl.ds(pl.multiple_of(base + q0, bq), bq)
            keys = pl.ds(pl.multiple_of(base + ws, half), win)
            s = lax.dot_general(qs[own, :], ks[keys, :], (((1,), (1,)), ((), ())),
                                preferred_element_type=_F32) + bias[(q0 - ws) // half, :, 0:win]
            m = jnp.broadcast_to(jnp.max(s, axis=-1, keepdims=True), (bq, HEAD_DIM))
            if prev is not None:
                m_p = prev[1, own, :]
                m = jnp.maximum(m, m_p)
            e = jnp.concatenate([jnp.exp2(s[:, c:c + HEAD_DIM] - m) for c in range(0, win, HEAD_DIM)],
                                axis=1)
            pv = jnp.dot(e.astype(_BF16), vs[keys, :], preferred_element_type=_F32)
            acc, l = pv[:, 0:HEAD_DIM], pv[:, HEAD_DIM:]
            if prev is not None:
                a = jnp.exp2(m_p - m)
                acc = a * prev[0, own, :] + acc
                l = a * prev[2, own, :] + l
            if store is None:
                o_ref[0, own, :] = (acc / l).astype(o_ref.dtype)
            else:
                pd, plen = parent
                ratio = dil // pd
                dst = pl.ds((r % pd) * plen + r // pd + ratio * q0, bq, stride=ratio)
                store[0, dst, :] = acc
                store[1, dst, :] = m
                store[2, dst, :] = l

        def body(tt, carry):
            for g in range(group):
                block(tt * group + g)
            return carry

        lax.fori_loop(0, seq // (bq * group), body, 0)

    branch(w2, d2, (d1, len1), None, run1)
    cast_rows(lambda sl: qm[sl, :], lambda sl: km[sl, :], lambda sl: vm[sl, :])
    branch(w1, d1, (d0, seq), run1, run0)
    cast_rows(lambda sl: qn[sl, :], lambda sl: kn[sl, :], lambda sl: v_ref[0, sl, :])
    branch(w0, d0, None, run0, None)


def _attn_call(proj3, cosf, sinf, a_heads):
    b, s, _ = proj3.shape
    bq = 128
    group = min(32, s // bq)
    assert (s // bq) % group == 0
    half_max = max(w // (2 * d) for w, d in DILATED_PATTERNS)
    col = lambda off: pl.BlockSpec((1, s, HEAD_DIM), lambda bi, h, off=off: (bi, 0, off + h))
    rope_spec = pl.BlockSpec((1, s, HEAD_DIM), lambda bi, h: (bi, 0, 0), pipeline_mode=pl.Buffered(1))
    seq_f32 = pltpu.VMEM((s, HEAD_DIM), _F32)
    seq_bf16 = pltpu.VMEM((s, HEAD_DIM), _BF16)
    return pl.pallas_call(
        functools.partial(_attn_kernel, seq=s, bq=bq, group=group),
        grid=(b, a_heads),
        in_specs=[col(0), col(a_heads), col(2 * a_heads), rope_spec, rope_spec],
        out_specs=pl.BlockSpec((1, s, HEAD_DIM), lambda bi, h: (bi, 0, h)),
        out_shape=jax.ShapeDtypeStruct((b, s, a_heads * HEAD_DIM), _BF16),
        scratch_shapes=[seq_f32] * 5 + [seq_bf16] * 2 + [pltpu.VMEM((s, 2 * HEAD_DIM), _BF16)]
        + [pltpu.VMEM((3, s, HEAD_DIM), _F32)] * 2
        + [pltpu.VMEM((3, bq, bq + 2 * half_max), _F32)],
        compiler_params=_params(("parallel", "parallel")),
    )(proj3, proj3, proj3, cosf, sinf)


def _scan_rows(x, reverse):
    n = x.shape[0]
    row = lax.broadcasted_iota(jnp.int32, x.shape, 0)
    k = 1
    while k < n:
        if reverse:
            x = x + jnp.where(row < n - k, pltpu.roll(x, n - k, axis=0), 0.0)
        else:
            x = x + jnp.where(row >= k, pltpu.roll(x, k, axis=0), 0.0)
        k *= 2
    return x


def _hgrn_chunk(qc, vc16, fp, lb, state_t, reverse):
    c = HGRN_CHUNK
    f = lb + (1.0 - lb) * jax.nn.sigmoid(fp)
    kk = 1.0 - f
    bc = _scan_rows(jnp.log2(f), reverse)
    inter = lax.dot_general((qc * jnp.exp2(bc)).astype(_BF16), state_t.astype(_BF16),
                            (((1,), (1,)), ((), ())), preferred_element_type=_F32)
    blocks = []
    for i in range(c // HGRN_SUB):
        lo, hi = i * HGRN_SUB, (i + 1) * HGRN_SUB
        beta = bc[hi - 1:hi] if reverse else bc[lo:lo + 1]
        qt = qc[lo:hi] * jnp.exp2(bc[lo:hi] - beta)
        kt = kk * jnp.exp2(jnp.minimum(beta - bc, HGRN_EXP2_CLAMP))
        blocks.append(lax.dot_general(qt.astype(_BF16), kt.astype(_BF16),
                                      (((1,), (1,)), ((), ())), preferred_element_type=_F32))
    scores = jnp.concatenate(blocks, axis=0)
    t_idx = lax.broadcasted_iota(jnp.int32, (c, c), 0)
    s_idx = lax.broadcasted_iota(jnp.int32, (c, c), 1)
    keep = (s_idx >= t_idx) if reverse else (s_idx <= t_idx)
    scores = jnp.where(keep, scores, 0.0)
    intra = jnp.dot(scores.astype(_BF16), vc16, preferred_element_type=_F32)
    last = bc[0:1] if reverse else bc[c - 1:c]
    kd = (kk * jnp.exp2(last - bc)).astype(_BF16)
    new_state = jnp.exp2(last) * state_t + lax.dot_general(
        vc16, kd, (((0,), (0,)), ((), ())), preferred_element_type=_F32)
    return inter + intra, new_state


def _hgrn_kernel(q_ref, i_ref, g_ref, ff_ref, fb_ref, gf_ref, gb_ref, ng_ref, o_ref, yf, yb, *, seq):
    c = HGRN_CHUNK
    n = seq // c

    def lower_bound(gam_ref):
        gam = gam_ref[...]
        e = jnp.exp(gam - jnp.max(gam, axis=0, keepdims=True))
        return e[0:1] / jnp.sum(e, axis=0, keepdims=True)

    lb_f = lower_bound(gf_ref)
    lb_b = lower_bound(gb_ref)

    def step(t, carry):
        st_f, st_b = carry
        sl_f = pl.ds(pl.multiple_of(t * c, c), c)
        sl_b = pl.ds(pl.multiple_of((n - 1 - t) * c, c), c)
        y, st_f = _hgrn_chunk(jax.nn.silu(q_ref[0, sl_f, :]), i_ref[0, sl_f, :].astype(_BF16),
                              ff_ref[0, sl_f, :], lb_f, st_f, False)
        yf[sl_f, :] = y
        y, st_b = _hgrn_chunk(jax.nn.silu(q_ref[0, sl_b, :]), i_ref[0, sl_b, :].astype(_BF16),
                              fb_ref[0, sl_b, :], lb_b, st_b, True)
        yb[sl_b, :] = y
        return st_f, st_b

    zero = jnp.zeros((HEAD_DIM, HEAD_DIM), _F32)
    lax.fori_loop(0, n, step, (zero, zero), unroll=8)

    rows = 256

    def finish(t, carry):
        sl = pl.ds(pl.multiple_of(t * rows, rows), rows)
        y = yf[sl, :] + yb[sl, :]
        y = y * lax.rsqrt(jnp.mean(jnp.square(y), axis=-1, keepdims=True) + RMS_EPS)
        o_ref[0, sl, :] = (y * ng_ref[...] * jax.nn.silu(g_ref[0, sl, :])).astype(o_ref.dtype)
        return carry

    lax.fori_loop(0, seq // rows, finish, 0)


def _hgrn_call(proj3, gamma_f, gamma_b, norm_g, col0, b_heads):
    b, s, _ = proj3.shape
    col = lambda off: pl.BlockSpec((1, s, HEAD_DIM), lambda bi, h, off=off: (bi, 0, col0 + off + h))
    gam_spec = pl.BlockSpec((gamma_f.shape[0], HEAD_DIM), lambda bi, h: (0, h))
    return pl.pallas_call(
        functools.partial(_hgrn_kernel, seq=s),
        grid=(b, b_heads),
        in_specs=[col(0), col(b_heads), col(2 * b_heads), col(3 * b_heads), col(4 * b_heads),
                  gam_spec, gam_spec, pl.BlockSpec((1, HEAD_DIM), lambda bi, h: (0, h))],
        out_specs=pl.BlockSpec((1, s, HEAD_DIM), lambda bi, h: (bi, 0, h)),
        out_shape=jax.ShapeDtypeStruct((b, s, b_heads * HEAD_DIM), _BF16),
        scratch_shapes=[pltpu.VMEM((s, HEAD_DIM), _F32)] * 2,
        compiler_params=_params(("parallel", "parallel")),
    )(proj3, proj3, proj3, proj3, proj3, gamma_f, gamma_b, norm_g.reshape(1, -1))


def _memattn_kernel(q_ref, kv_ref, o_ref, *, m_width):
    hd = m_width // M_HEADS
    scale = hd ** -0.5
    for h in range(M_HEADS):
        q = (q_ref[0, :, h * hd:(h + 1) * hd] * scale).astype(_BF16)
        k = kv_ref[0, :, h * hd:(h + 1) * hd].astype(_BF16)
        v = kv_ref[0, :, m_width + h * hd:m_width + (h + 1) * hd].astype(_BF16)
        s = lax.dot_general(q, k, (((1,), (1,)), ((), ())), preferred_element_type=_F32)
        e = jnp.exp(s - jnp.max(s, axis=-1, keepdims=True))
        o = jnp.dot(e.astype(_BF16), v, preferred_element_type=_F32) / jnp.sum(e, axis=-1, keepdims=True)
        o_ref[0, :, h * hd:(h + 1) * hd] = o.astype(o_ref.dtype)


def _memattn_call(proj3, kvm, q_col_block, m_width):
    b, s, _ = proj3.shape
    mlen = kvm.shape[1]
    tq = _pick(s, (512, 256, 128))
    return pl.pallas_call(
        functools.partial(_memattn_kernel, m_width=m_width),
        grid=(b, s // tq),
        in_specs=[pl.BlockSpec((1, tq, m_width), lambda bi, i: (bi, i, q_col_block)),
                  pl.BlockSpec((1, mlen, 2 * m_width), lambda bi, i: (bi, 0, 0))],
        out_specs=pl.BlockSpec((1, tq, m_width), lambda bi, i: (bi, i, 0)),
        out_shape=jax.ShapeDtypeStruct((b, s, m_width), _BF16),
        compiler_params=_params(("parallel", "parallel")),
    )(proj3, kvm)


def _mix_out_kernel(a_ref, b_ref, m_ref, wa_ref, wb_ref, wm_ref, r_ref, o_ref, *, alpha):
    acc = jnp.dot(a_ref[...], wa_ref[...].astype(_BF16), preferred_element_type=_F32)
    acc += jnp.dot(b_ref[...], wb_ref[...].astype(_BF16), preferred_element_type=_F32)
    acc += jnp.dot(m_ref[...], wm_ref[...].astype(_BF16), preferred_element_type=_F32)
    o_ref[...] = alpha * r_ref[...] + acc


def _mix_out_call(oa, ob, om, w_o, resid, alpha):
    m, aw = oa.shape
    bw, mw = ob.shape[1], om.shape[1]
    n = w_o.shape[1]
    tm = _pick(m, (1024, 512, 256))
    tn = _pick(n, (512, 256, 128))
    assert aw == bw and (aw + bw) % mw == 0
    return pl.pallas_call(
        functools.partial(_mix_out_kernel, alpha=alpha),
        grid=(m // tm, n // tn),
        in_specs=[pl.BlockSpec((tm, aw), lambda i, j: (i, 0)),
                  pl.BlockSpec((tm, bw), lambda i, j: (i, 0)),
                  pl.BlockSpec((tm, mw), lambda i, j: (i, 0)),
                  pl.BlockSpec((aw, tn), lambda i, j: (0, j)),
                  pl.BlockSpec((bw, tn), lambda i, j: (1, j)),
                  pl.BlockSpec((mw, tn), lambda i, j: ((aw + bw) // mw, j)),
                  pl.BlockSpec((tm, tn), lambda i, j: (i, j))],
        out_specs=pl.BlockSpec((tm, tn), lambda i, j: (i, j)),
        out_shape=jax.ShapeDtypeStruct((m, n), _F32),
        compiler_params=_params(("parallel", "parallel")),
    )(oa, ob, om, w_o, w_o, w_o, resid)


def _ln_rows(y, g_ref, b_ref):
    mu = jnp.mean(y, axis=-1, keepdims=True)
    d = y - mu
    var = jnp.mean(d * d, axis=-1, keepdims=True)
    return d * lax.rsqrt(var + LN_EPS) * g_ref[...] + b_ref[...]


def _ln_kernel(*refs, alpha):
    if alpha is None:
        y_ref, g_ref, b_ref, *o_refs = refs
        y = y_ref[...]
    else:
        y_ref, pre_ref, rg_ref, rb_ref, g_ref, b_ref, *o_refs = refs
        y = alpha * _ln_rows(pre_ref[...], rg_ref, rb_ref) + y_ref[...]
    out = _ln_rows(y, g_ref, b_ref)
    for o_ref in o_refs:
        o_ref[...] = out.astype(o_ref.dtype)


def _layernorm_call(y, g, b, out_dtypes, resid_pre=None, resid_g=None, resid_b=None, alpha=None):
    m, n = y.shape
    tr = _pick(m, (256, 128, 64))
    row = pl.BlockSpec((tr, n), lambda i: (i, 0))
    vec = pl.BlockSpec((1, n), lambda i: (0, 0))
    if resid_pre is None:
        args, specs = (y,), [row]
    else:
        args, specs = (y, resid_pre, resid_g.reshape(1, n), resid_b.reshape(1, n)), [row, row, vec, vec]
    return pl.pallas_call(
        functools.partial(_ln_kernel, alpha=None if resid_pre is None else alpha),
        grid=(m // tr,),
        in_specs=specs + [vec, vec],
        out_specs=[row] * len(out_dtypes),
        out_shape=[jax.ShapeDtypeStruct((m, n), dt) for dt in out_dtypes],
        compiler_params=_params(("parallel",)),
    )(*args, g.reshape(1, n), b.reshape(1, n))


def _ffn_up_kernel(x_hbm, wg_ref, wu_ref, cw_ref, cb_ref, h_ref, lhs_ref, sem, *, tm, tiles_per_seq):
    i = pl.program_id(0)
    j = pl.program_id(1)

    @pl.when(j == 0)
    def _():
        pos = i % tiles_per_seq
        row0 = pl.multiple_of(i * tm, tm)
        main = pltpu.make_async_copy(x_hbm.at[pl.ds(row0, tm), :], lhs_ref.at[pl.ds(HALO, tm), :],
                                     sem.at[0])
        main.start()

        @pl.when(pos > 0)
        def _():
            cp = pltpu.make_async_copy(x_hbm.at[pl.ds(row0 - HALO, HALO), :],
                                       lhs_ref.at[pl.ds(0, HALO), :], sem.at[1])
            cp.start()
            cp.wait()

        @pl.when(pos == 0)
        def _():
            lhs_ref[0:HALO, :] = jnp.zeros((HALO, lhs_ref.shape[1]), lhs_ref.dtype)

        @pl.when(pos < tiles_per_seq - 1)
        def _():
            cp = pltpu.make_async_copy(x_hbm.at[pl.ds(row0 + tm, HALO), :],
                                       lhs_ref.at[pl.ds(HALO + tm, HALO), :], sem.at[2])
            cp.start()
            cp.wait()

        @pl.when(pos == tiles_per_seq - 1)
        def _():
            lhs_ref[HALO + tm:, :] = jnp.zeros((HALO, lhs_ref.shape[1]), lhs_ref.dtype)

        main.wait()

    ext = tm + 2 * HALO
    g = jnp.dot(lhs_ref[...], wg_ref[...].astype(_BF16), preferred_element_type=_F32)
    u = jnp.dot(lhs_ref[HALO:HALO + tm, :], wu_ref[...].astype(_BF16), preferred_element_type=_F32)
    g_prev = pltpu.roll(g, 1, axis=0)[HALO:HALO + tm]
    g_next = pltpu.roll(g, ext - 1, axis=0)[HALO:HALO + tm]
    gc = cb_ref[...] + g_prev * cw_ref[0:1, :] + g[HALO:HALO + tm] * cw_ref[1:2, :] + g_next * cw_ref[2:3, :]
    h_ref[...] = (jax.nn.silu(gc) * u).astype(h_ref.dtype)


def _ffn_up_call(x1b, w_up, conv_w, conv_b, seq):
    m, d = x1b.shape
    d_ff = w_up.shape[1] // 2
    tm = _pick(seq, (2048, 1024, 512, 256))
    tn = _pick(d_ff, (256, 128))
    nj = d_ff // tn
    return pl.pallas_call(
        functools.partial(_ffn_up_kernel, tm=tm, tiles_per_seq=seq // tm),
        grid=(m // tm, nj),
        in_specs=[pl.BlockSpec(memory_space=pl.ANY),
                  pl.BlockSpec((d, tn), lambda i, j: (0, j)),
                  pl.BlockSpec((d, tn), lambda i, j: (0, j + nj)),
                  pl.BlockSpec((CONV_WIDTH, tn), lambda i, j: (0, j)),
                  pl.BlockSpec((1, tn), lambda i, j: (0, j))],
        out_specs=pl.BlockSpec((tm, tn), lambda i, j: (i, j)),
        out_shape=jax.ShapeDtypeStruct((m, d_ff), _BF16),
        scratch_shapes=[pltpu.VMEM((tm + 2 * HALO, d), _BF16), pltpu.SemaphoreType.DMA((3,))],
        compiler_params=_params(("arbitrary", "arbitrary")),
    )(x1b, w_up, w_up, conv_w, conv_b.reshape(1, d_ff))


def _rope_tables(positions):
    inv = 1.0 / (ROPE_THETA ** (jnp.arange(0, HEAD_DIM, 2, dtype=_F32) / HEAD_DIM))
    ang = positions.astype(_F32)[..., None] * inv
    cos, sin = jnp.cos(ang), jnp.sin(ang)
    return jnp.concatenate([cos, cos], axis=-1), jnp.concatenate([-sin, sin], axis=-1)


def kernel(x, mem, positions, w_in, w_mem_kv, w_o, hgrn_gamma_fwd, hgrn_gamma_bwd, hgrn_norm_g,
           ln1_g, ln1_b, w_up, conv_w, conv_b, w_down, ln2_g, ln2_b):
    b, s, d = x.shape
    depth = w_in.shape[0]
    a_width = b_width = 3 * d // 8
    m_width = d // 4
    a_heads = a_width // HEAD_DIM
    b_heads = b_width // HEAD_DIM
    alpha = (2 * depth) ** 0.25
    m = b * s
    cosf, sinf = _rope_tables(positions)

    xf = x.reshape(m, d)
    for layer in range(depth):
        assert layer == 0, "lower-bound running sum is implemented for the first layer only"
        xb = xf.astype(_BF16)
        proj = _matmul_rows_resident(xb, w_in[layer], _F32, _pick(m, (2048, 1024, 512, 256)),
                                     _pick(w_in.shape[2], (512, 256, 128)))
        proj3 = proj.reshape(b, s, -1)
        o_a = _attn_call(proj3, cosf, sinf, a_heads)
        o_b = _hgrn_call(proj3, hgrn_gamma_fwd, hgrn_gamma_bwd, hgrn_norm_g[layer],
                         3 * a_heads, b_heads)
        kvm = _matmul(mem.reshape(-1, d), w_mem_kv[layer], _F32,
                      _pick(b * mem.shape[1], (512, 256, 128)), _pick(2 * m_width, (512, 256, 128)))
        o_m = _memattn_call(proj3, kvm.reshape(b, mem.shape[1], 2 * m_width),
                            (3 * a_width + 5 * b_width) // m_width, m_width)
        y1 = _mix_out_call(o_a.reshape(m, a_width), o_b.reshape(m, b_width), o_m.reshape(m, m_width),
                           w_o[layer], xf, alpha)
        (x1b,) = _layernorm_call(y1, ln1_g[layer], ln1_b[layer], (_BF16,))
        h = _ffn_up_call(x1b, w_up[layer], conv_w[layer], conv_b[layer], s)
        ffn = _matmul_ktiled(h, w_down[layer], _pick(m, (2048, 1024, 512, 256)),
                             _pick(d, (1024, 512, 256)), 1024)
        (xf,) = _layernorm_call(ffn, ln2_g[layer], ln2_b[layer], (_F32,), resid_pre=y1,
                                resid_g=ln1_g[layer], resid_b=ln1_b[layer], alpha=alpha)
    return xf.reshape(b, s, d)
```
